```python
import jax, jax.numpy as jnp
from jax import lax
import numpy as np

D_MODEL = 1024
BATCH = 4
SEQ = 4096
DEPTH = 2

CHUNK = 64
N_MIXERS = 2
EPS = 1e-6

GMLP_BLOCK = 128
GMLP_D_FFN = 6 * D_MODEL
GMLP_HALF = GMLP_D_FFN // 2
GMLP_GROUPS = 8
GMLP_GROUP_DIM = GMLP_HALF // GMLP_GROUPS

ATT_HEADS = 16
ATT_HEAD_DIM = D_MODEL // ATT_HEADS
ATT_LEFT_CHUNKS = 8
ATT_BAND = (ATT_LEFT_CHUNKS + 1) * CHUNK
REL_CLIP = 256
REL_SIZE = REL_CLIP + CHUNK

FFN_DIM = 2816
N_EXPERTS = 8
TOP_K = 2
EXPERT_DIM = 3584

N_A = (DEPTH + 1) // 2
N_B = DEPTH // 2

kernel_name = "hybrid_gmlp_bandattn_moe_streaming"


def rmsnorm(x, g):
    xf = x.astype(jnp.float32)
    y = xf * lax.rsqrt(jnp.mean(xf * xf, axis=-1, keepdims=True) + EPS)
    return (y * g.astype(jnp.float32)).astype(x.dtype)


def swiglu(h, w_gate, w_up, w_down):
    return (jax.nn.silu(h @ w_gate) * (h @ w_up)) @ w_down


def gmlp_mixer(h, w_in, v_gain, w_s, b_s, w_out):
    B, S, _ = h.shape
    z = jax.nn.gelu(h @ w_in, approximate=False)
    u, v = jnp.split(z, 2, axis=-1)
    v = rmsnorm(v, v_gain)
    nb = S // GMLP_BLOCK
    v = v.reshape(B, nb, GMLP_BLOCK, GMLP_GROUPS, GMLP_GROUP_DIM)
    chunk_id = jnp.arange(GMLP_BLOCK) // CHUNK
    mask = chunk_id[:, None] >= chunk_id[None, :]
    w = jnp.where(mask[None], w_s, jnp.zeros_like(w_s))
    sv = jnp.einsum('gts,bnsgc->bntgc', w, v) + b_s.T[None, None, :, :, None]
    gated = u * sv.reshape(B, S, GMLP_HALF)
    return gated @ w_out


def band_attention(h, w_qkv, rel_table, w_o):
    B, S, _ = h.shape
    nc = S // CHUNK
    qkv = (h @ w_qkv).reshape(B, S, 3, ATT_HEADS, ATT_HEAD_DIM)
    q, k, v = qkv[:, :, 0], qkv[:, :, 1], qkv[:, :, 2]
    pad = ATT_LEFT_CHUNKS * CHUNK
    k_pad = jnp.pad(k, ((0, 0), (pad, 0), (0, 0), (0, 0)))
    v_pad = jnp.pad(v, ((0, 0), (pad, 0), (0, 0), (0, 0)))
    q_chunks = q.reshape(B, nc, CHUNK, ATT_HEADS, ATT_HEAD_DIM).transpose(1, 0, 2, 3, 4)
    qi = jnp.arange(CHUNK)[:, None]
    kj = jnp.arange(ATT_BAND)
    dist = pad + qi - kj[None, :]
    idx = jnp.clip(dist, -(CHUNK - 1), REL_CLIP) + (CHUNK - 1)
    bias = rel_table[:, idx].astype(jnp.float32)
    scale = ATT_HEAD_DIM ** -0.5

    def one_chunk(args):
        n, q_n = args
        k_b = lax.dynamic_slice_in_dim(k_pad, n * CHUNK, ATT_BAND, axis=1)
        v_b = lax.dynamic_slice_in_dim(v_pad, n * CHUNK, ATT_BAND, axis=1)
        s = jnp.einsum('bqhd,bkhd->bhqk', q_n, k_b).astype(jnp.float32) * scale + bias
        valid = (n - ATT_LEFT_CHUNKS) * CHUNK + kj >= 0
        s = jnp.where(valid, s, -1e30)
        p = jax.nn.softmax(s, axis=-1).astype(v_b.dtype)
        return jnp.einsum('bhqk,bkhd->bqhd', p, v_b)

    o = lax.map(one_chunk, (jnp.arange(nc), q_chunks))
    o = o.transpose(1, 0, 2, 3, 4).reshape(B, S, D_MODEL)
    return o @ w_o


def moe_swiglu(h, router_w, router_b, w_gate, w_up, w_down):
    B, S, D = h.shape
    t = h.reshape(-1, D)
    logits = (t @ router_w).astype(jnp.float32) + router_b.astype(jnp.float32)
    top_vals, top_idx = lax.top_k(logits, TOP_K)
    top_w = jax.nn.softmax(top_vals, axis=-1)
    gates = jnp.sum(jax.nn.one_hot(top_idx, N_EXPERTS, dtype=jnp.float32) * top_w[..., None],
                    axis=1).astype(h.dtype)
    y = jnp.zeros_like(t)
    for e in range(N_EXPERTS):
        y = y + gates[:, e:e + 1] * swiglu(t, w_gate[e], w_up[e], w_down[e])
    return y.reshape(B, S, D)


def setup_inputs(seed: int = 0) -> dict:
    key = jax.random.key(seed)
    ks = jax.random.split(key, 24)
    f32 = jnp.float32

    def nrm(k, shape, scale):
        return jax.random.normal(k, shape, f32) * scale

    D = D_MODEL
    return {
        "x": nrm(ks[0], (BATCH, SEQ, D), 1.0),
        "norm_mix_g": 1.0 + nrm(ks[1], (DEPTH, D), 0.02),
        "norm_ffn_g": 1.0 + nrm(ks[2], (DEPTH, D), 0.02),
        "final_g": 1.0 + nrm(ks[3], (D,), 0.02),
        "a_w_in": nrm(ks[4], (N_A, D, GMLP_D_FFN), D ** -0.5),
        "a_v_gain": 1.0 + nrm(ks[5], (N_A, GMLP_HALF), 0.02),
        "a_w_s": nrm(ks[6], (N_A, GMLP_GROUPS, GMLP_BLOCK, GMLP_BLOCK), GMLP_BLOCK ** -0.5),
        "a_b_s": 1.0 + nrm(ks[7], (N_A, GMLP_GROUPS, GMLP_BLOCK), 0.02),
        "a_w_out": nrm(ks[8], (N_A, GMLP_HALF, D), GMLP_HALF ** -0.5),
        "a_ffn_gate": nrm(ks[9], (N_A, D, FFN_DIM), D ** -0.5),
        "a_ffn_up": nrm(ks[10], (N_A, D, FFN_DIM), D ** -0.5),
        "a_ffn_down": nrm(ks[11], (N_A, FFN_DIM, D), FFN_DIM ** -0.5),
        "b_w_qkv": nrm(ks[12], (N_B, D, 3 * D), D ** -0.5),
        "b_rel_bias": nrm(ks[13], (N_B, ATT_HEADS, REL_SIZE), 0.1),
        "b_w_o": nrm(ks[14], (N_B, D, D), D ** -0.5),
        "b_router_w": nrm(ks[15], (N_B, D, N_EXPERTS), D ** -0.5),
        "b_router_b": nrm(ks[16], (N_B, N_EXPERTS), 0.01),
        "b_exp_gate": nrm(ks[17], (N_B, N_EXPERTS, D, EXPERT_DIM), D ** -0.5),
        "b_exp_up": nrm(ks[18], (N_B, N_EXPERTS, D, EXPERT_DIM), D ** -0.5),
        "b_exp_down": nrm(ks[19], (N_B, N_EXPERTS, EXPERT_DIM, D), EXPERT_DIM ** -0.5),
    }


def reference(x, norm_mix_g, norm_ffn_g, final_g,
              a_w_in, a_v_gain, a_w_s, a_b_s, a_w_out,
              a_ffn_gate, a_ffn_up, a_ffn_down,
              b_w_qkv, b_rel_bias, b_w_o,
              b_router_w, b_router_b, b_exp_gate, b_exp_up, b_exp_down):
    for i in range(DEPTH):
        j = i // N_MIXERS
        h = rmsnorm(x, norm_mix_g[i])
        if i % N_MIXERS == 0:
            x = x + gmlp_mixer(h, a_w_in[j], a_v_gain[j], a_w_s[j], a_b_s[j], a_w_out[j])
        else:
            x = x + band_attention(h, b_w_qkv[j], b_rel_bias[j], b_w_o[j])
        h = rmsnorm(x, norm_ffn_g[i])
        if i % 2 == 0:
            x = x + swiglu(h, a_ffn_gate[j], a_ffn_up[j], a_ffn_down[j])
        else:
            x = x + moe_swiglu(h, b_router_w[j], b_router_b[j],
                               b_exp_gate[j], b_exp_up[j], b_exp_down[j])
    return rmsnorm(x, final_g)
```

```python
import functools

import jax
import jax.numpy as jnp
from jax import lax
from jax.experimental import pallas as pl
from jax.experimental.pallas import tpu as pltpu

EPS = 1e-6
CHUNK = 64
GMLP_BLOCK = 128
GMLP_GROUPS = 8
ATT_HEADS = 16
ATT_LEFT_CHUNKS = 8
REL_CLIP = 256
N_EXPERTS = 8
NEG_INF = -1e30

VMEM_LIMIT_BYTES = 56 * 1024 * 1024

F32 = jnp.float32
BF16 = jnp.bfloat16


def _params(semantics):
    return pltpu.CompilerParams(dimension_semantics=semantics,
                                vmem_limit_bytes=VMEM_LIMIT_BYTES)


def _resident(shape):
    zeros = (0,) * len(shape)
    return pl.BlockSpec(shape, lambda *_: zeros, pipeline_mode=pl.Buffered(1))


def _rmsnorm(x, g):
    ms = jnp.mean(x * x, axis=-1, keepdims=True)
    return x * lax.rsqrt(ms + EPS) * g


def _gelu_exact(z):
    return 0.5 * z * (1.0 + lax.erf(z * (2.0 ** -0.5)))


def _gmlp_kernel(x_ref, g_ref, win_ref, vgain_ref, ws_ref, bst_ref, wout_ref,
                 o_ref, u_scr, v_scr, gated_scr, *, n_chunk):
    tm = x_ref.shape[0]
    half = u_scr.shape[1]
    group_dim = half // GMLP_GROUPS
    x = x_ref[...]
    h = _rmsnorm(x, g_ref[...]).astype(BF16)

    ssq = jnp.zeros((tm, 1), F32)
    for c in range(0, 2 * half, n_chunk):
        z = jnp.dot(h, win_ref[:, c:c + n_chunk], preferred_element_type=F32)
        z = _gelu_exact(z)
        if c < half:
            u_scr[:, c:c + n_chunk] = z
        else:
            v_scr[:, c - half:c - half + n_chunk] = z
            ssq = ssq + jnp.sum(z * z, axis=-1, keepdims=True)
    inv = lax.rsqrt(ssq * (1.0 / half) + EPS)

    row_chunk = lax.broadcasted_iota(jnp.int32, (GMLP_BLOCK, GMLP_BLOCK), 0) // CHUNK
    col_chunk = lax.broadcasted_iota(jnp.int32, (GMLP_BLOCK, GMLP_BLOCK), 1) // CHUNK
    causal = row_chunk >= col_chunk
    for g in range(GMLP_GROUPS):
        cols = slice(g * group_dim, (g + 1) * group_dim)
        w_g = jnp.where(causal, ws_ref[g], 0.0).astype(BF16)
        b_g = bst_ref[:, g:g + 1]
        for n in range(tm // GMLP_BLOCK):
            rows = slice(n * GMLP_BLOCK, (n + 1) * GMLP_BLOCK)
            vn = (v_scr[rows, cols] * inv[rows] * vgain_ref[:, cols]).astype(BF16)
            sv = jnp.dot(w_g, vn, preferred_element_type=F32) + b_g
            gated_scr[rows, cols] = (u_scr[rows, cols] * sv).astype(BF16)

    out = jnp.dot(gated_scr[...], wout_ref[...], preferred_element_type=F32)
    o_ref[...] = x + out


def _gmlp_layer(x2d, g, w_in, v_gain, w_s, b_s_t, w_out, *, tm=256, n_chunk=512):
    t, d = x2d.shape
    half = w_out.shape[0]
    return pl.pallas_call(
        functools.partial(_gmlp_kernel, n_chunk=n_chunk),
        out_shape=jax.ShapeDtypeStruct((t, d), F32),
        grid=(t // tm,),
        in_specs=[
            pl.BlockSpec((tm, d), lambda i: (i, 0)),
            _resident((1, d)),
            _resident(w_in.shape),
            _resident((1, half)),
            _resident(w_s.shape),
            _resident(b_s_t.shape),
            _resident(w_out.shape),
        ],
        out_specs=pl.BlockSpec((tm, d), lambda i: (i, 0)),
        scratch_shapes=[
            pltpu.VMEM((tm, half), F32),
            pltpu.VMEM((tm, half), F32),
            pltpu.VMEM((tm, half), BF16),
        ],
        compiler_params=_params(("parallel",)),
        name="gmlp_mixer",
    )(x2d, g, w_in, v_gain, w_s, b_s_t, w_out)


def _swiglu_kernel(x_ref, g_ref, wg_ref, wu_ref, wd_ref, o_ref, *, f_chunk):
    x = x_ref[...]
    h = _rmsnorm(x, g_ref[...]).astype(BF16)
    o_ref[...] = x
    for c in range(0, wg_ref.shape[1], f_chunk):
        a = jnp.dot(h, wg_ref[:, c:c + f_chunk], preferred_element_type=F32)
        b = jnp.dot(h, wu_ref[:, c:c + f_chunk], preferred_element_type=F32)
        act = (jax.nn.silu(a) * b).astype(BF16)
        o_ref[...] += jnp.dot(act, wd_ref[c:c + f_chunk, :], preferred_element_type=F32)


def _swiglu_layer(x2d, g, w_gate, w_up, w_down, *, tm=256, f_chunk=256):
    t, d = x2d.shape
    return pl.pallas_call(
        functools.partial(_swiglu_kernel, f_chunk=f_chunk),
        out_shape=jax.ShapeDtypeStruct((t, d), F32),
        grid=(t // tm,),
        in_specs=[
            pl.BlockSpec((tm, d), lambda i: (i, 0)),
            _resident((1, d)),
            _resident(w_gate.shape),
            _resident(w_up.shape),
            _resident(w_down.shape),
        ],
        out_specs=pl.BlockSpec((tm, d), lambda i: (i, 0)),
        compiler_params=_params(("parallel",)),
        name="dense_swiglu",
    )(x2d, g, w_gate, w_up, w_down)


def _qkv_kernel(x_ref, g_ref, w_ref, q_ref, k_ref, v_ref, *, q_scale):
    d = x_ref.shape[1]
    h = _rmsnorm(x_ref[...], g_ref[...]).astype(BF16)
    q = jnp.dot(h, w_ref[:, 0:d], preferred_element_type=F32)
    q_ref[...] = (q * q_scale).astype(BF16)
    k_ref[...] = jnp.dot(h, w_ref[:, d:2 * d], preferred_element_type=F32).astype(BF16)
    v_ref[...] = jnp.dot(h, w_ref[:, 2 * d:3 * d], preferred_element_type=F32).astype(BF16)


def _qkv_layer(x2d, g, w_qkv, *, q_scale, tm=512):
    t, d = x2d.shape
    row_spec = pl.BlockSpec((tm, d), lambda i: (i, 0))
    out = jax.ShapeDtypeStruct((t, d), BF16)
    return pl.pallas_call(
        functools.partial(_qkv_kernel, q_scale=q_scale),
        out_shape=(out, out, out),
        grid=(t // tm,),
        in_specs=[row_spec, _resident((1, d)), _resident(w_qkv.shape)],
        out_specs=(row_spec, row_spec, row_spec),
        compiler_params=_params(("parallel",)),
        name="qkv_proj",
    )(x2d, g, w_qkv)


def _attn_kernel(x_ref, q_ref, k0_ref, k1_ref, k2_ref, v0_ref, v1_ref, v2_ref,
                 bias_ref, wo_ref, o_ref, ctx_scr, *, head_dim):
    tq = q_ref.shape[0]
    i = pl.program_id(1)
    k_refs = (k0_ref, k1_ref, k2_ref)
    v_refs = (v0_ref, v1_ref, v2_ref)
    n_win = len(k_refs) * tq
    key_pos = lax.broadcasted_iota(jnp.int32, (1, n_win), 1) + (i - 2) * tq
    in_seq = key_pos >= 0
    nt_dims = (((1,), (1,)), ((), ()))
    for h in range(ATT_HEADS):
        lanes = slice(h * head_dim, (h + 1) * head_dim)
        qh = q_ref[:, lanes]
        s = jnp.concatenate(
            [lax.dot_general(qh, kr[:, lanes], nt_dims, preferred_element_type=F32)
             for kr in k_refs], axis=1)
        s = jnp.where(in_seq, s + bias_ref[h], NEG_INF)
        m = jnp.max(s, axis=-1, keepdims=True)
        p = jnp.exp(s - m)
        l = jnp.sum(p, axis=-1, keepdims=True)
        pb = p.astype(BF16)
        ctx = jnp.zeros((tq, head_dim), F32)
        for j, vr in enumerate(v_refs):
            ctx = ctx + jnp.dot(pb[:, j * tq:(j + 1) * tq], vr[:, lanes],
                                preferred_element_type=F32)
        ctx_scr[:, lanes] = (ctx / l).astype(BF16)
    o_ref[...] = x_ref[...] + jnp.dot(ctx_scr[...], wo_ref[...],
                                      preferred_element_type=F32)


def _attn_layer(x2d, q, k, v, bias, w_o, *, batch, tq=256):
    t, d = x2d.shape
    nt = t // batch // tq

    def cur(b, i):
        return (b * nt + i, 0)

    def back(n):
        return lambda b, i: (b * nt + jnp.maximum(i - n, 0), 0)

    blk = lambda f: pl.BlockSpec((tq, d), f)
    return pl.pallas_call(
        functools.partial(_attn_kernel, head_dim=d // ATT_HEADS),
        out_shape=jax.ShapeDtypeStruct((t, d), F32),
        grid=(batch, nt),
        in_specs=[blk(cur), blk(cur),
                  blk(back(2)), blk(back(1)), blk(cur),
                  blk(back(2)), blk(back(1)), blk(cur),
                  _resident(bias.shape), _resident(w_o.shape)],
        out_specs=blk(cur),
        scratch_shapes=[pltpu.VMEM((tq, d), BF16)],
        compiler_params=_params(("parallel", "arbitrary")),
        name="band_attention",
    )(x2d, q, k, k, k, v, v, v, bias, w_o)


def _band_bias(rel_table, tq):
    band = (ATT_LEFT_CHUNKS + 1) * CHUNK
    r = jnp.arange(tq)[:, None]
    c = jnp.arange(3 * tq)[None, :]
    kj = c - (r // CHUNK) * CHUNK
    dist = ATT_LEFT_CHUNKS * CHUNK + (r % CHUNK) - kj
    idx = jnp.clip(dist, -(CHUNK - 1), REL_CLIP) + (CHUNK - 1)
    in_band = (kj >= 0) & (kj < band)
    return jnp.where(in_band[None], rel_table[:, idx].astype(F32), NEG_INF)


def _route_top2(logits):
    n_e = logits.shape[-1]
    lane = lax.broadcasted_iota(jnp.int32, logits.shape, 1)
    m1 = jnp.max(logits, axis=-1, keepdims=True)
    i1 = jnp.min(jnp.where(logits == m1, lane, n_e), axis=-1, keepdims=True)
    rest = jnp.where(lane == i1, -jnp.inf, logits)
    m2 = jnp.max(rest, axis=-1, keepdims=True)
    i2 = jnp.min(jnp.where(rest == m2, lane, n_e), axis=-1, keepdims=True)
    e2 = jnp.exp(m2 - m1)
    denom = 1.0 + e2
    return jnp.where(lane == i1, 1.0 / denom, 0.0) + jnp.where(lane == i2, e2 / denom, 0.0)


def _moe_kernel(x_ref, g_ref, rw_ref, rb_ref, wg_ref, wu_ref, wd_ref, fg_ref,
                o_ref, h_scr, gate_scr):
    e = pl.program_id(1)
    f = pl.program_id(2)

    @pl.when((e == 0) & (f == 0))
    def _():
        x = x_ref[...]
        hf = _rmsnorm(x, g_ref[...])
        h_scr[...] = hf.astype(BF16)
        logits = jnp.dot(hf, rw_ref[...], preferred_element_type=F32,
                         precision=lax.Precision.HIGHEST) + rb_ref[...]
        gate_scr[...] = _route_top2(logits)
        o_ref[...] = x

    gates = gate_scr[...]
    lane = lax.broadcasted_iota(jnp.int32, gates.shape, 1)
    gate_e = jnp.sum(jnp.where(lane == e, gates, 0.0), axis=-1, keepdims=True)
    h = h_scr[...]
    a = jnp.dot(h, wg_ref[...], preferred_element_type=F32)
    b = jnp.dot(h, wu_ref[...], preferred_element_type=F32)
    act = (jax.nn.silu(a) * b).astype(BF16)
    o_ref[...] += gate_e * jnp.dot(act, wd_ref[...], preferred_element_type=F32)

    @pl.when((e == pl.num_programs(1) - 1) & (f == pl.num_programs(2) - 1))
    def _():
        o_ref[...] = _rmsnorm(o_ref[...], fg_ref[...])


def _moe_layer(x2d, g, router_w, router_b, w_gate, w_up, w_down, final_g, *, tm=512, f_chunk=512):
    t, d = x2d.shape
    n_e, _, f_dim = w_gate.shape
    return pl.pallas_call(
        _moe_kernel,
        out_shape=jax.ShapeDtypeStruct((t, d), F32),
        grid=(t // tm, n_e, f_dim // f_chunk),
        in_specs=[
            pl.BlockSpec((tm, d), lambda i, e, f: (i, 0)),
            _resident((1, d)),
            _resident(router_w.shape),
            _resident((1, n_e)),
            pl.BlockSpec((None, d, f_chunk), lambda i, e, f: (e, 0, f)),
            pl.BlockSpec((None, d, f_chunk), lambda i, e, f: (e, 0, f)),
            pl.BlockSpec((None, f_chunk, d), lambda i, e, f: (e, f, 0)),
            _resident((1, d)),
        ],
        out_specs=pl.BlockSpec((tm, d), lambda i, e, f: (i, 0)),
        scratch_shapes=[pltpu.VMEM((tm, d), BF16), pltpu.VMEM((tm, n_e), F32)],
        compiler_params=_params(("parallel", "arbitrary", "arbitrary")),
        name="moe_swiglu",
    )(x2d, g, router_w, router_b, w_gate, w_up, w_down, final_g)


def kernel(x, norm_mix_g, norm_ffn_g, final_g, a_w_in, a_v_gain, a_w_s, a_b_s, a_w_out,
           a_ffn_gate, a_ffn_up, a_ffn_down, b_w_qkv, b_rel_bias, b_w_o, b_router_w,
           b_router_b, b_exp_gate, b_exp_up, b_exp_down):
    batch, seq, d = x.shape
    depth = norm_mix_g.shape[0]
    head_dim = d // ATT_HEADS
    tq = 256
    xf = x.reshape(batch * seq, d)
    row = lambda vec: vec.reshape(1, -1).astype(F32)

    for layer in range(depth):
        j = layer // 2
        if layer % 2 == 0:
            xf = _gmlp_layer(xf, row(norm_mix_g[layer]), a_w_in[j].astype(BF16),
                             row(a_v_gain[j]), a_w_s[j], a_b_s[j].T,
                             a_w_out[j].astype(BF16))
            xf = _swiglu_layer(xf, row(norm_ffn_g[layer]), a_ffn_gate[j].astype(BF16),
                               a_ffn_up[j].astype(BF16), a_ffn_down[j].astype(BF16))
        else:
            q, k, v = _qkv_layer(xf, row(norm_mix_g[layer]), b_w_qkv[j].astype(BF16),
                                 q_scale=head_dim ** -0.5)
            bias = _band_bias(b_rel_bias[j], tq)
            xf = _attn_layer(xf, q, k, v, bias, b_w_o[j].astype(BF16), batch=batch, tq=tq)
            last = layer == depth - 1
            xf = _moe_layer(xf, row(norm_ffn_g[layer]), b_router_w[j], row(b_router_b[j]),
                            b_exp_gate[j].astype(BF16), b_exp_up[j].astype(BF16),
                            b_exp_down[j].astype(BF16),
                            row(final_g) if last else None)
    return xf.reshape(batch, seq, d)
```

```python
import functools

import jax
import jax.numpy as jnp
from jax import lax
from jax.experimental import pallas as pl
from jax.experimental.pallas import tpu as pltpu

EPS = 1e-6
CHUNK = 64
GMLP_BLOCK = 128
GMLP_GROUPS = 8
ATT_HEADS = 16
ATT_LEFT_CHUNKS = 8
REL_CLIP = 256
TOP_K = 2
NEG_INF = -1e30

VMEM_LIMIT_BYTES = 56 * 1024 * 1024

F32 = jnp.float32
BF16 = jnp.bfloat16


def _params(semantics):
    return pltpu.CompilerParams(dimension_semantics=semantics,
                                vmem_limit_bytes=VMEM_LIMIT_BYTES)


def _resident(shape):
    zeros = (0,) * len(shape)
    return pl.BlockSpec(shape, lambda *_: zeros, pipeline_mode=pl.Buffered(1))


def _rmsnorm(x, g):
    ms = jnp.mean(x * x, axis=-1, keepdims=True)
    return x * lax.rsqrt(ms + EPS) * g


def _gelu_exact(z):
    return 0.5 * z * (1.0 + lax.erf(z * (2.0 ** -0.5)))


def _gmlp_kernel(x_ref, g_ref, win_ref, vgain_ref, ws_ref, bst_ref, wout_ref,
                 o_ref, u_scr, v_scr, gated_scr, *, n_chunk):
    tm = x_ref.shape[0]
    half = u_scr.shape[1]
    group_dim = half // GMLP_GROUPS
    x = x_ref[...]
    h = _rmsnorm(x, g_ref[...]).astype(BF16)

    ssq = jnp.zeros((tm, 1), F32)
    for c in range(0, 2 * half, n_chunk):
        z = jnp.dot(h, win_ref[:, c:c + n_chunk], preferred_element_type=F32)
        z = _gelu_exact(z)
        if c < half:
            u_scr[:, c:c + n_chunk] = z
        else:
            v_scr[:, c - half:c - half + n_chunk] = z
            ssq = ssq + jnp.sum(z * z, axis=-1, keepdims=True)
    inv = lax.rsqrt(ssq * (1.0 / half) + EPS)

    row_chunk = lax.broadcasted_iota(jnp.int32, (GMLP_BLOCK, GMLP_BLOCK), 0) // CHUNK
    col_chunk = lax.broadcasted_iota(jnp.int32, (GMLP_BLOCK, GMLP_BLOCK), 1) // CHUNK
    causal = row_chunk >= col_chunk
    for g in range(GMLP_GROUPS):
        cols = slice(g * group_dim, (g + 1) * group_dim)
        w_g = jnp.where(causal, ws_ref[g], 0.0).astype(BF16)
        b_g = bst_ref[:, g:g + 1]
        for n in range(tm // GMLP_BLOCK):
            rows = slice(n * GMLP_BLOCK, (n + 1) * GMLP_BLOCK)
            vn = (v_scr[rows, cols] * inv[rows] * vgain_ref[:, cols]).astype(BF16)
            sv = jnp.dot(w_g, vn, preferred_element_type=F32) + b_g
            gated_scr[rows, cols] = (u_scr[rows, cols] * sv).astype(BF16)

    out = jnp.dot(gated_scr[...], wout_ref[...], preferred_element_type=F32)
    o_ref[...] = x + out


def _gmlp_layer(x2d, g, w_in, v_gain, w_s, b_s_t, w_out, *, tm=256, n_chunk=512):
    t, d = x2d.shape
    half = w_out.shape[0]
    return pl.pallas_call(
        functools.partial(_gmlp_kernel, n_chunk=n_chunk),
        out_shape=jax.ShapeDtypeStruct((t, d), F32),
        grid=(t // tm,),
        in_specs=[
            pl.BlockSpec((tm, d), lambda i: (i, 0)),
            _resident((1, d)),
            _resident(w_in.shape),
            _resident((1, half)),
            _resident(w_s.shape),
            _resident(b_s_t.shape),
            _resident(w_out.shape),
        ],
        out_specs=pl.BlockSpec((tm, d), lambda i: (i, 0)),
        scratch_shapes=[
            pltpu.VMEM((tm, half), F32),
            pltpu.VMEM((tm, half), F32),
            pltpu.VMEM((tm, half), BF16),
        ],
        compiler_params=_params(("parallel",)),
        name="gmlp_mixer",
    )(x2d, g, w_in, v_gain, w_s, b_s_t, w_out)


def _swiglu_kernel(x_ref, g_ref, wg_ref, wu_ref, wd_ref, o_ref, *, f_chunk):
    x = x_ref[...]
    h = _rmsnorm(x, g_ref[...]).astype(BF16)
    o_ref[...] = x
    for c in range(0, wg_ref.shape[1], f_chunk):
        a = jnp.dot(h, wg_ref[:, c:c + f_chunk], preferred_element_type=F32)
        b = jnp.dot(h, wu_ref[:, c:c + f_chunk], preferred_element_type=F32)
        act = (jax.nn.silu(a) * b).astype(BF16)
        o_ref[...] += jnp.dot(act, wd_ref[c:c + f_chunk, :], preferred_element_type=F32)


def _swiglu_layer(x2d, g, w_gate, w_up, w_down, *, tm=256, f_chunk=256):
    t, d = x2d.shape
    return pl.pallas_call(
        functools.partial(_swiglu_kernel, f_chunk=f_chunk),
        out_shape=jax.ShapeDtypeStruct((t, d), F32),
        grid=(t // tm,),
        in_specs=[
            pl.BlockSpec((tm, d), lambda i: (i, 0)),
            _resident((1, d)),
            _resident(w_gate.shape),
            _resident(w_up.shape),
            _resident(w_down.shape),
        ],
        out_specs=pl.BlockSpec((tm, d), lambda i: (i, 0)),
        compiler_params=_params(("parallel",)),
        name="dense_swiglu",
    )(x2d, g, w_gate, w_up, w_down)


def _qkv_kernel(x_ref, g_ref, w_ref, q_ref, k_ref, v_ref, *, q_scale):
    d = x_ref.shape[1]
    h = _rmsnorm(x_ref[...], g_ref[...]).astype(BF16)
    q = jnp.dot(h, w_ref[:, 0:d], preferred_element_type=F32)
    q_ref[...] = (q * q_scale).astype(BF16)
    k_ref[...] = jnp.dot(h, w_ref[:, d:2 * d], preferred_element_type=F32).astype(BF16)
    v_ref[...] = jnp.dot(h, w_ref[:, 2 * d:3 * d], preferred_element_type=F32).astype(BF16)


def _qkv_layer(x2d, g, w_qkv, *, q_scale, tm=512):
    t, d = x2d.shape
    row_spec = pl.BlockSpec((tm, d), lambda i: (i, 0))
    out = jax.ShapeDtypeStruct((t, d), BF16)
    return pl.pallas_call(
        functools.partial(_qkv_kernel, q_scale=q_scale),
        out_shape=(out, out, out),
        grid=(t // tm,),
        in_specs=[row_spec, _resident((1, d)), _resident(w_qkv.shape)],
        out_specs=(row_spec, row_spec, row_spec),
        compiler_params=_params(("parallel",)),
        name="qkv_proj",
    )(x2d, g, w_qkv)


def _attn_kernel(x_ref, q_ref, k0_ref, k1_ref, k2_ref, v0_ref, v1_ref, v2_ref,
                 bias_ref, wo_ref, o_ref, ctx_scr, *, head_dim):
    tq = q_ref.shape[0]
    i = pl.program_id(1)
    k_refs = (k0_ref, k1_ref, k2_ref)
    v_refs = (v0_ref, v1_ref, v2_ref)
    n_win = len(k_refs) * tq
    key_pos = lax.broadcasted_iota(jnp.int32, (1, n_win), 1) + (i - 2) * tq
    in_seq = key_pos >= 0
    nt_dims = (((1,), (1,)), ((), ()))
    for h in range(ATT_HEADS):
        lanes = slice(h * head_dim, (h + 1) * head_dim)
        qh = q_ref[:, lanes]
        s = jnp.concatenate(
            [lax.dot_general(qh, kr[:, lanes], nt_dims, preferred_element_type=F32)
             for kr in k_refs], axis=1)
        s = jnp.where(in_seq, s + bias_ref[h], NEG_INF)
        m = jnp.max(s, axis=-1, keepdims=True)
        p = jnp.exp(s - m)
        l = jnp.sum(p, axis=-1, keepdims=True)
        pb = p.astype(BF16)
        ctx = jnp.zeros((tq, head_dim), F32)
        for j, vr in enumerate(v_refs):
            ctx = ctx + jnp.dot(pb[:, j * tq:(j + 1) * tq], vr[:, lanes],
                                preferred_element_type=F32)
        ctx_scr[:, lanes] = (ctx / l).astype(BF16)
    o_ref[...] = x_ref[...] + jnp.dot(ctx_scr[...], wo_ref[...],
                                      preferred_element_type=F32)


def _attn_layer(x2d, q, k, v, bias, w_o, *, batch, tq=256):
    t, d = x2d.shape
    nt = t // batch // tq

    def cur(b, i):
        return (b * nt + i, 0)

    def back(n):
        return lambda b, i: (b * nt + jnp.maximum(i - n, 0), 0)

    blk = lambda f: pl.BlockSpec((tq, d), f)
    return pl.pallas_call(
        functools.partial(_attn_kernel, head_dim=d // ATT_HEADS),
        out_shape=jax.ShapeDtypeStruct((t, d), F32),
        grid=(batch, nt),
        in_specs=[blk(cur), blk(cur),
                  blk(back(2)), blk(back(1)), blk(cur),
                  blk(back(2)), blk(back(1)), blk(cur),
                  _resident(bias.shape), _resident(w_o.shape)],
        out_specs=blk(cur),
        scratch_shapes=[pltpu.VMEM((tq, d), BF16)],
        compiler_params=_params(("parallel", "arbitrary")),
        name="band_attention",
    )(x2d, q, k, k, k, v, v, v, bias, w_o)


def _band_bias(rel_table, tq):
    band = (ATT_LEFT_CHUNKS + 1) * CHUNK
    n_heads = rel_table.shape[0]
    assert 3 * tq >= band + tq - CHUNK and rel_table.shape[1] == REL_CLIP + CHUNK
    tail = jnp.broadcast_to(rel_table[:, -1:], (n_heads, band - 1 - REL_CLIP))
    rev = jnp.concatenate([rel_table, tail], axis=1).astype(F32)[:, ::-1]
    chunk = jnp.stack([rev[:, CHUNK - 1 - qi:CHUNK - 1 - qi + band] for qi in range(CHUNK)],
                      axis=1)
    tiles = [jnp.pad(chunk, ((0, 0), (0, 0), (c * CHUNK, 3 * tq - band - c * CHUNK)),
                     constant_values=NEG_INF) for c in range(tq // CHUNK)]
    return jnp.concatenate(tiles, axis=1)


def _route_kernel(x_ref, g_ref, rw_ref, rb_ref, idx_ref, rank_ref, w_ref, cnt_ref, run_scr):
    tm = x_ref.shape[0]

    @pl.when(pl.program_id(0) == 0)
    def _():
        run_scr[...] = jnp.zeros_like(run_scr)

    hf = _rmsnorm(x_ref[...], g_ref[...])
    logits = jnp.dot(hf, rw_ref[...], preferred_element_type=F32,
                     precision=lax.Precision.HIGHEST) + rb_ref[...]
    n_e = logits.shape[-1]
    lane = lax.broadcasted_iota(jnp.int32, logits.shape, 1)
    m1 = jnp.max(logits, axis=-1, keepdims=True)
    i1 = jnp.min(jnp.where(logits == m1, lane, n_e), axis=-1, keepdims=True)
    rest = jnp.where(lane == i1, -jnp.inf, logits)
    m2 = jnp.max(rest, axis=-1, keepdims=True)
    i2 = jnp.min(jnp.where(rest == m2, lane, n_e), axis=-1, keepdims=True)
    e2 = jnp.exp(m2 - m1)
    denom = 1.0 + e2

    sel = jnp.where((lane == i1) | (lane == i2), 1.0, 0.0)
    before = (lax.broadcasted_iota(jnp.int32, (tm, tm), 0)
              > lax.broadcasted_iota(jnp.int32, (tm, tm), 1))
    cum = jnp.dot(jnp.where(before, 1.0, 0.0).astype(BF16), sel.astype(BF16),
                  preferred_element_type=F32) + run_scr[...]
    r1 = jnp.sum(jnp.where(lane == i1, cum, 0.0), axis=-1, keepdims=True)
    r2 = jnp.sum(jnp.where(lane == i2, cum, 0.0), axis=-1, keepdims=True)
    run_scr[...] += jnp.sum(sel, axis=0, keepdims=True)

    idx_ref[...] = jnp.concatenate([i1, i2], axis=1)
    rank_ref[...] = jnp.concatenate([r1, r2], axis=1).astype(jnp.int32)
    w_ref[...] = jnp.concatenate([1.0 / denom, e2 / denom], axis=1)
    cnt_ref[...] = run_scr[...].astype(jnp.int32)


def _route(x2d, g, router_w, router_b, *, tm=512):
    t, d = x2d.shape
    n_e = router_w.shape[1]
    pair = pl.BlockSpec((tm, TOP_K), lambda i: (i, 0))
    return pl.pallas_call(
        _route_kernel,
        out_shape=(jax.ShapeDtypeStruct((t, TOP_K), jnp.int32),
                   jax.ShapeDtypeStruct((t, TOP_K), jnp.int32),
                   jax.ShapeDtypeStruct((t, TOP_K), F32),
                   jax.ShapeDtypeStruct((1, n_e), jnp.int32)),
        grid=(t // tm,),
        in_specs=[pl.BlockSpec((tm, d), lambda i: (i, 0)), _resident((1, d)),
                  _resident(router_w.shape), _resident((1, n_e))],
        out_specs=(pair, pair, pair, pl.BlockSpec((1, n_e), lambda i: (0, 0))),
        scratch_shapes=[pltpu.VMEM((1, n_e), F32)],
        compiler_params=_params(("arbitrary",)),
        name="moe_route",
    )(x2d, g, router_w, router_b)


def _wait_rows(hbm_ref, vmem_slot, sem, n_rows):
    pltpu.make_async_copy(hbm_ref.at[pl.ds(0, n_rows)], vmem_slot, sem).wait()


def _dispatch_kernel(pos_ref, x_ref, g_ref, zeros_ref, hs_ref, h_scr, sem):
    del zeros_ref
    i = pl.program_id(0)
    n = pl.num_programs(0)
    tm = x_ref.shape[0]
    slot = lax.rem(i, 2)

    def wait_slot(s):
        for _ in range(TOP_K):
            _wait_rows(hs_ref, h_scr.at[s], sem.at[s], tm)

    @pl.when(i >= 2)
    def _():
        wait_slot(slot)

    h_scr[slot] = _rmsnorm(x_ref[...], g_ref[...])
    base = i * (TOP_K * tm)

    def issue(r, carry):
        for k in range(TOP_K):
            p = pos_ref[base + TOP_K * r + k]
            pltpu.make_async_copy(h_scr.at[slot, pl.ds(r, 1)], hs_ref.at[pl.ds(p, 1)],
                                  sem.at[slot]).start()
        return carry

    lax.fori_loop(0, tm, issue, 0, unroll=8)

    @pl.when(i == n - 1)
    def _():
        wait_slot(slot)

    @pl.when((i == n - 1) & (n >= 2))
    def _():
        wait_slot(1 - slot)


def _dispatch(pos, x2d, g, n_rows, *, tm=256):
    t, d = x2d.shape
    zeros = jnp.zeros((n_rows, d), F32)
    return pl.pallas_call(
        _dispatch_kernel,
        out_shape=jax.ShapeDtypeStruct((n_rows, d), F32),
        grid_spec=pltpu.PrefetchScalarGridSpec(
            num_scalar_prefetch=1,
            grid=(t // tm,),
            in_specs=[pl.BlockSpec((tm, d), lambda i, pos: (i, 0)),
                      pl.BlockSpec((1, d), lambda i, pos: (0, 0)),
                      pl.BlockSpec(memory_space=pl.ANY)],
            out_specs=pl.BlockSpec(memory_space=pl.ANY),
            scratch_shapes=[pltpu.VMEM((2, tm, d), F32), pltpu.SemaphoreType.DMA((2,))],
        ),
        input_output_aliases={3: 0},
        compiler_params=_params(("arbitrary",)),
        name="moe_dispatch",
    )(pos, x2d, g, zeros)


def _experts_kernel(te_ref, nused_ref, hs_ref, wg_ref, wu_ref, wd_ref, y_ref, hb_scr):
    del te_ref
    i = pl.program_id(0)
    f = pl.program_id(1)

    @pl.when(i < nused_ref[0])
    def _():
        @pl.when(f == 0)
        def _():
            hb_scr[...] = hs_ref[...].astype(BF16)

        h = hb_scr[...]
        a = jnp.dot(h, wg_ref[...], preferred_element_type=F32)
        b = jnp.dot(h, wu_ref[...], preferred_element_type=F32)
        act = (jax.nn.silu(a) * b).astype(BF16)
        y = jnp.dot(act, wd_ref[...], preferred_element_type=F32)

        @pl.when(f == 0)
        def _():
            y_ref[...] = y

        @pl.when(f > 0)
        def _():
            y_ref[...] += y

    @pl.when((i >= nused_ref[0]) & (f == 0))
    def _():
        y_ref[...] = jnp.zeros_like(y_ref)


def _experts(tile_expert, n_used, hs, w_gate, w_up, w_down, *, tm, f_chunk=512):
    n_rows, d = hs.shape
    f_dim = w_gate.shape[2]
    n_f = f_dim // f_chunk

    def row_blk(i, f, te, nu):
        return (jnp.minimum(i, nu[0] - 1), 0)

    def f_blk(i, f, nu):
        return jnp.where(i < nu[0], f, n_f - 1)

    return pl.pallas_call(
        _experts_kernel,
        out_shape=jax.ShapeDtypeStruct((n_rows, d), F32),
        grid_spec=pltpu.PrefetchScalarGridSpec(
            num_scalar_prefetch=2,
            grid=(n_rows // tm, n_f),
            in_specs=[
                pl.BlockSpec((tm, d), row_blk),
                pl.BlockSpec((None, d, f_chunk), lambda i, f, te, nu: (te[i], 0, f_blk(i, f, nu))),
                pl.BlockSpec((None, d, f_chunk), lambda i, f, te, nu: (te[i], 0, f_blk(i, f, nu))),
                pl.BlockSpec((None, f_chunk, d), lambda i, f, te, nu: (te[i], f_blk(i, f, nu), 0)),
            ],
            out_specs=pl.BlockSpec((tm, d), lambda i, f, te, nu: (i, 0)),
            scratch_shapes=[pltpu.VMEM((tm, d), BF16)],
        ),
        compiler_params=_params(("arbitrary", "arbitrary")),
        name="moe_experts",
    )(tile_expert, n_used, hs, w_gate, w_up, w_down)


def _combine_kernel(pos_ref, x_ref, w_ref, fg_ref, y_ref, o_ref, buf, sem):
    i = pl.program_id(0)
    n = pl.num_programs(0)
    tm = x_ref.shape[0]
    slot = lax.rem(i, 2)

    def gather(tile, s):
        base = tile * (TOP_K * tm)

        def issue(r, carry):
            for k in range(TOP_K):
                p = pos_ref[base + TOP_K * r + k]
                pltpu.make_async_copy(y_ref.at[pl.ds(p, 1)], buf.at[s, k, pl.ds(r, 1)],
                                      sem.at[s]).start()
            return carry

        lax.fori_loop(0, tm, issue, 0, unroll=8)

    @pl.when(i == 0)
    def _():
        gather(0, 0)

    @pl.when(i + 1 < n)
    def _():
        gather(i + 1, 1 - slot)

    for k in range(TOP_K):
        _wait_rows(y_ref, buf.at[slot, k], sem.at[slot], tm)

    w = w_ref[...]
    y = x_ref[...] + (w[:, 0:1] * buf[slot, 0] + w[:, 1:2] * buf[slot, 1])
    o_ref[...] = _rmsnorm(y, fg_ref[...])


def _combine(pos, x2d, w, final_g, y_sorted, *, tm=256):
    t, d = x2d.shape
    return pl.pallas_call(
        _combine_kernel,
        out_shape=jax.ShapeDtypeStruct((t, d), F32),
        grid_spec=pltpu.PrefetchScalarGridSpec(
            num_scalar_prefetch=1,
            grid=(t // tm,),
            in_specs=[pl.BlockSpec((tm, d), lambda i, pos: (i, 0)),
                      pl.BlockSpec((tm, TOP_K), lambda i, pos: (i, 0)),
                      pl.BlockSpec((1, d), lambda i, pos: (0, 0)),
                      pl.BlockSpec(memory_space=pl.ANY)],
            out_specs=pl.BlockSpec((tm, d), lambda i, pos: (i, 0)),
            scratch_shapes=[pltpu.VMEM((2, TOP_K, tm, d), F32), pltpu.SemaphoreType.DMA((2,))],
        ),
        compiler_params=_params(("arbitrary",)),
        name="moe_combine",
    )(pos, x2d, w, final_g, y_sorted)


def _moe_layer(x2d, g, router_w, router_b, w_gate, w_up, w_down, final_g, *, row_tile=512):
    t, d = x2d.shape
    n_e = router_w.shape[1]
    idx, rank, w, counts = _route(x2d, g, router_w, router_b)

    counts = counts[0]
    padded = (counts + row_tile - 1) // row_tile * row_tile
    ends = jnp.cumsum(padded)
    starts = ends - padded
    experts = jnp.arange(n_e, dtype=jnp.int32)
    pos = rank + jnp.sum(jnp.where(idx[..., None] == experts, starts, 0), axis=-1)
    pos = pos.reshape(-1).astype(jnp.int32)

    n_tiles = (TOP_K * t) // row_tile + n_e
    tile_ends = ends // row_tile
    n_used = tile_ends[-1:].astype(jnp.int32)
    tile_ids = jnp.arange(n_tiles, dtype=jnp.int32)
    tile_expert = jnp.sum(jnp.minimum(tile_ids, n_used - 1)[:, None] >= tile_ends[None, :],
                          axis=1).astype(jnp.int32)

    hs = _dispatch(pos, x2d, g, n_tiles * row_tile)
    y_sorted = _experts(tile_expert, n_used, hs, w_gate, w_up, w_down, tm=row_tile)
    return _combine(pos, x2d, w, final_g, y_sorted)


def kernel(x, norm_mix_g, norm_ffn_g, final_g, a_w_in, a_v_gain, a_w_s, a_b_s, a_w_out,
           a_ffn_gate, a_ffn_up, a_ffn_down, b_w_qkv, b_rel_bias, b_w_o, b_router_w,
           b_router_b, b_exp_gate, b_exp_up, b_exp_down):
    batch, seq, d = x.shape
    assert norm_mix_g.shape[0] == 2, "two layers: gMLP/SwiGLU then attention/MoE"
    head_dim = d // ATT_HEADS
    tq = 256
    xf = x.reshape(batch * seq, d)
    row = lambda vec: vec.reshape(1, -1).astype(F32)

    xf = _gmlp_layer(xf, row(norm_mix_g[0]), a_w_in[0].astype(BF16), row(a_v_gain[0]),
                     a_w_s[0], a_b_s[0].T, a_w_out[0].astype(BF16))
    xf = _swiglu_layer(xf, row(norm_ffn_g[0]), a_ffn_gate[0].astype(BF16),
                       a_ffn_up[0].astype(BF16), a_ffn_down[0].astype(BF16))

    q, k, v = _qkv_layer(xf, row(norm_mix_g[1]), b_w_qkv[0].astype(BF16),
                         q_scale=head_dim ** -0.5)
    bias = _band_bias(b_rel_bias[0], tq)
    xf = _attn_layer(xf, q, k, v, bias, b_w_o[0].astype(BF16), batch=batch, tq=tq)
    xf = _moe_layer(xf, row(norm_ffn_g[1]), b_router_w[0], row(b_router_b[0]),
                    b_exp_gate[0].astype(BF16), b_exp_up[0].astype(BF16),
                    b_exp_down[0].astype(BF16), row(final_g))
    return xf.reshape(batch, seq, d)
```

```python
import functools

import jax
import jax.numpy as jnp
from jax import lax
from jax.experimental import pallas as pl
from jax.experimental.pallas import tpu as pltpu

EPS = 1e-6
CHUNK = 64
GMLP_BLOCK = 128
GMLP_GROUPS = 8
ATT_HEADS = 16
ATT_LEFT_CHUNKS = 8
REL_CLIP = 256
TOP_K = 2
NEG_INF = -1e30

VMEM_LIMIT_BYTES = 56 * 1024 * 1024
EXPERTS_VMEM_LIMIT_BYTES = 60 * 1024 * 1024

F32 = jnp.float32
BF16 = jnp.bfloat16


def _params(semantics):
    return pltpu.CompilerParams(dimension_semantics=semantics,
                                vmem_limit_bytes=VMEM_LIMIT_BYTES)


def _resident(shape):
    zeros = (0,) * len(shape)
    return pl.BlockSpec(shape, lambda *_: zeros, pipeline_mode=pl.Buffered(1))


def _rmsnorm(x, g):
    ms = jnp.mean(x * x, axis=-1, keepdims=True)
    return x * lax.rsqrt(ms + EPS) * g


def _gelu_exact(z):
    return 0.5 * z * (1.0 + lax.erf(z * (2.0 ** -0.5)))


def _gmlp_kernel(x_ref, g_ref, win_ref, vgain_ref, ws_ref, bst_ref, wout_ref,
                 o_ref, u_scr, v_scr, gated_scr, *, n_chunk):
    tm = x_ref.shape[0]
    half = u_scr.shape[1]
    group_dim = half // GMLP_GROUPS
    x = x_ref[...]
    h = _rmsnorm(x, g_ref[...]).astype(BF16)

    ssq = jnp.zeros((tm, 1), F32)
    for c in range(0, 2 * half, n_chunk):
        z = jnp.dot(h, win_ref[:, c:c + n_chunk], preferred_element_type=F32)
        z = _gelu_exact(z)
        if c < half:
            u_scr[:, c:c + n_chunk] = z
        else:
            v_scr[:, c - half:c - half + n_chunk] = z
            ssq = ssq + jnp.sum(z * z, axis=-1, keepdims=True)
    inv = lax.rsqrt(ssq * (1.0 / half) + EPS)

    row_chunk = lax.broadcasted_iota(jnp.int32, (GMLP_BLOCK, GMLP_BLOCK), 0) // CHUNK
    col_chunk = lax.broadcasted_iota(jnp.int32, (GMLP_BLOCK, GMLP_BLOCK), 1) // CHUNK
    causal = row_chunk >= col_chunk
    for g in range(GMLP_GROUPS):
        cols = slice(g * group_dim, (g + 1) * group_dim)
        w_g = jnp.where(causal, ws_ref[g], 0.0).astype(BF16)
        b_g = bst_ref[:, g:g + 1]
        for n in range(tm // GMLP_BLOCK):
            rows = slice(n * GMLP_BLOCK, (n + 1) * GMLP_BLOCK)
            vn = (v_scr[rows, cols] * inv[rows] * vgain_ref[:, cols]).astype(BF16)
            sv = jnp.dot(w_g, vn, preferred_element_type=F32) + b_g
            gated_scr[rows, cols] = (u_scr[rows, cols] * sv).astype(BF16)

    out = jnp.dot(gated_scr[...], wout_ref[...], preferred_element_type=F32)
    o_ref[...] = x + out


def _gmlp_layer(x2d, g, w_in, v_gain, w_s, b_s_t, w_out, *, tm=256, n_chunk=512):
    t, d = x2d.shape
    half = w_out.shape[0]
    return pl.pallas_call(
        functools.partial(_gmlp_kernel, n_chunk=n_chunk),
        out_shape=jax.ShapeDtypeStruct((t, d), F32),
        grid=(t // tm,),
        in_specs=[
            pl.BlockSpec((tm, d), lambda i: (i, 0)),
            _resident((1, d)),
            _resident(w_in.shape),
            _resident((1, half)),
            _resident(w_s.shape),
            _resident(b_s_t.shape),
            _resident(w_out.shape),
        ],
        out_specs=pl.BlockSpec((tm, d), lambda i: (i, 0)),
        scratch_shapes=[
            pltpu.VMEM((tm, half), F32),
            pltpu.VMEM((tm, half), F32),
            pltpu.VMEM((tm, half), BF16),
        ],
        compiler_params=_params(("parallel",)),
        name="gmlp_mixer",
    )(x2d, g, w_in, v_gain, w_s, b_s_t, w_out)


def _swiglu_kernel(x_ref, g_ref, wg_ref, wu_ref, wd_ref, o_ref, *, f_chunk):
    x = x_ref[...]
    h = _rmsnorm(x, g_ref[...]).astype(BF16)
    o_ref[...] = x
    for c in range(0, wg_ref.shape[1], f_chunk):
        a = jnp.dot(h, wg_ref[:, c:c + f_chunk], preferred_element_type=F32)
        b = jnp.dot(h, wu_ref[:, c:c + f_chunk], preferred_element_type=F32)
        act = (jax.nn.silu(a) * b).astype(BF16)
        o_ref[...] += jnp.dot(act, wd_ref[c:c + f_chunk, :], preferred_element_type=F32)


def _swiglu_layer(x2d, g, w_gate, w_up, w_down, *, tm=256, f_chunk=256):
    t, d = x2d.shape
    return pl.pallas_call(
        functools.partial(_swiglu_kernel, f_chunk=f_chunk),
        out_shape=jax.ShapeDtypeStruct((t, d), F32),
        grid=(t // tm,),
        in_specs=[
            pl.BlockSpec((tm, d), lambda i: (i, 0)),
            _resident((1, d)),
            _resident(w_gate.shape),
            _resident(w_up.shape),
            _resident(w_down.shape),
        ],
        out_specs=pl.BlockSpec((tm, d), lambda i: (i, 0)),
        compiler_params=_params(("parallel",)),
        name="dense_swiglu",
    )(x2d, g, w_gate, w_up, w_down)


def _qkv_kernel(x_ref, g_ref, w_ref, q_ref, k_ref, v_ref, *, q_scale):
    d = x_ref.shape[1]
    h = _rmsnorm(x_ref[...], g_ref[...]).astype(BF16)
    q = jnp.dot(h, w_ref[:, 0:d], preferred_element_type=F32)
    q_ref[...] = (q * q_scale).astype(BF16)
    k_ref[...] = jnp.dot(h, w_ref[:, d:2 * d], preferred_element_type=F32).astype(BF16)
    v_ref[...] = jnp.dot(h, w_ref[:, 2 * d:3 * d], preferred_element_type=F32).astype(BF16)


def _qkv_layer(x2d, g, w_qkv, *, q_scale, tm=512):
    t, d = x2d.shape
    row_spec = pl.BlockSpec((tm, d), lambda i: (i, 0))
    out = jax.ShapeDtypeStruct((t, d), BF16)
    return pl.pallas_call(
        functools.partial(_qkv_kernel, q_scale=q_scale),
        out_shape=(out, out, out),
        grid=(t // tm,),
        in_specs=[row_spec, _resident((1, d)), _resident(w_qkv.shape)],
        out_specs=(row_spec, row_spec, row_spec),
        compiler_params=_params(("parallel",)),
        name="qkv_proj",
    )(x2d, g, w_qkv)


def _attn_kernel(x_ref, q_ref, k0_ref, k1_ref, k2_ref, v0_ref, v1_ref, v2_ref,
                 bias_ref, wo_ref, o_ref, ctx_scr, *, head_dim):
    tq = q_ref.shape[0]
    i = pl.program_id(1)
    k_refs = (k0_ref, k1_ref, k2_ref)
    v_refs = (v0_ref, v1_ref, v2_ref)
    n_win = len(k_refs) * tq
    key_pos = lax.broadcasted_iota(jnp.int32, (1, n_win), 1) + (i - 2) * tq
    in_seq = key_pos >= 0
    nt_dims = (((1,), (1,)), ((), ()))

    def scores(h):
        lanes = slice(h * head_dim, (h + 1) * head_dim)
        qh = q_ref[:, lanes]
        return jnp.concatenate(
            [lax.dot_general(qh, kr[:, lanes], nt_dims, preferred_element_type=F32)
             for kr in k_refs], axis=1)

    s_next = scores(0)
    for h in range(ATT_HEADS):
        lanes = slice(h * head_dim, (h + 1) * head_dim)
        s = s_next
        if h + 1 < ATT_HEADS:
            s_next = scores(h + 1)
        s = jnp.where(in_seq, s + bias_ref[h], NEG_INF)
        m = jnp.max(s, axis=-1, keepdims=True)
        p = jnp.exp(s - m)
        l = jnp.sum(p, axis=-1, keepdims=True)
        pb = p.astype(BF16)
        ctx = jnp.zeros((tq, head_dim), F32)
        for j, vr in enumerate(v_refs):
            ctx = ctx + jnp.dot(pb[:, j * tq:(j + 1) * tq], vr[:, lanes],
                                preferred_element_type=F32)
        ctx_scr[:, lanes] = (ctx / l).astype(BF16)
    o_ref[...] = x_ref[...] + jnp.dot(ctx_scr[...], wo_ref[...],
                                      preferred_element_type=F32)


def _attn_layer(x2d, q, k, v, bias, w_o, *, batch, tq=256):
    t, d = x2d.shape
    nt = t // batch // tq

    def cur(b, i):
        return (b * nt + i, 0)

    def back(n):
        return lambda b, i: (b * nt + jnp.maximum(i - n, 0), 0)

    blk = lambda f: pl.BlockSpec((tq, d), f)
    return pl.pallas_call(
        functools.partial(_attn_kernel, head_dim=d // ATT_HEADS),
        out_shape=jax.ShapeDtypeStruct((t, d), F32),
        grid=(batch, nt),
        in_specs=[blk(cur), blk(cur),
                  blk(back(2)), blk(back(1)), blk(cur),
                  blk(back(2)), blk(back(1)), blk(cur),
                  _resident(bias.shape), _resident(w_o.shape)],
        out_specs=blk(cur),
        scratch_shapes=[pltpu.VMEM((tq, d), BF16)],
        compiler_params=_params(("parallel", "arbitrary")),
        name="band_attention",
    )(x2d, q, k, k, k, v, v, v, bias, w_o)


def _band_bias(rel_table, tq):
    band = (ATT_LEFT_CHUNKS + 1) * CHUNK
    n_heads = rel_table.shape[0]
    assert 3 * tq >= band + tq - CHUNK and rel_table.shape[1] == REL_CLIP + CHUNK
    tail = jnp.broadcast_to(rel_table[:, -1:], (n_heads, band - 1 - REL_CLIP))
    rev = jnp.concatenate([rel_table, tail], axis=1).astype(F32)[:, ::-1]
    chunk = jnp.stack([rev[:, CHUNK - 1 - qi:CHUNK - 1 - qi + band] for qi in range(CHUNK)],
                      axis=1)
    tiles = [jnp.pad(chunk, ((0, 0), (0, 0), (c * CHUNK, 3 * tq - band - c * CHUNK)),
                     constant_values=NEG_INF) for c in range(tq // CHUNK)]
    return jnp.concatenate(tiles, axis=1)


def _route_kernel(x_ref, g_ref, rw_ref, rb_ref, idx_ref, rank_ref, w_ref, cnt_ref, run_scr):
    tm = x_ref.shape[0]

    @pl.when(pl.program_id(0) == 0)
    def _():
        run_scr[...] = jnp.zeros_like(run_scr)

    hf = _rmsnorm(x_ref[...], g_ref[...])
    logits = jnp.dot(hf, rw_ref[...], preferred_element_type=F32,
                     precision=lax.Precision.HIGHEST) + rb_ref[...]
    n_e = logits.shape[-1]
    lane = lax.broadcasted_iota(jnp.int32, logits.shape, 1)
    m1 = jnp.max(logits, axis=-1, keepdims=True)
    i1 = jnp.min(jnp.where(logits == m1, lane, n_e), axis=-1, keepdims=True)
    rest = jnp.where(lane == i1, -jnp.inf, logits)
    m2 = jnp.max(rest, axis=-1, keepdims=True)
    i2 = jnp.min(jnp.where(rest == m2, lane, n_e), axis=-1, keepdims=True)
    e2 = jnp.exp(m2 - m1)
    denom = 1.0 + e2

    sel = jnp.where((lane == i1) | (lane == i2), 1.0, 0.0)
    before = (lax.broadcasted_iota(jnp.int32, (tm, tm), 0)
              > lax.broadcasted_iota(jnp.int32, (tm, tm), 1))
    cum = jnp.dot(jnp.where(before, 1.0, 0.0).astype(BF16), sel.astype(BF16),
                  preferred_element_type=F32) + run_scr[...]
    r1 = jnp.sum(jnp.where(lane == i1, cum, 0.0), axis=-1, keepdims=True)
    r2 = jnp.sum(jnp.where(lane == i2, cum, 0.0), axis=-1, keepdims=True)
    run_scr[...] += jnp.sum(sel, axis=0, keepdims=True)

    idx_ref[...] = jnp.concatenate([i1, i2], axis=1)
    rank_ref[...] = jnp.concatenate([r1, r2], axis=1).astype(jnp.int32)
    w_ref[...] = jnp.concatenate([1.0 / denom, e2 / denom], axis=1)
    cnt_ref[...] = run_scr[...].astype(jnp.int32)


def _route(x2d, g, router_w, router_b, *, tm=512):
    t, d = x2d.shape
    n_e = router_w.shape[1]
    pair = pl.BlockSpec((tm, TOP_K), lambda i: (i, 0))
    return pl.pallas_call(
        _route_kernel,
        out_shape=(jax.ShapeDtypeStruct((t, TOP_K), jnp.int32),
                   jax.ShapeDtypeStruct((t, TOP_K), jnp.int32),
                   jax.ShapeDtypeStruct((t, TOP_K), F32),
                   jax.ShapeDtypeStruct((1, n_e), jnp.int32)),
        grid=(t // tm,),
        in_specs=[pl.BlockSpec((tm, d), lambda i: (i, 0)), _resident((1, d)),
                  _resident(router_w.shape), _resident((1, n_e))],
        out_specs=(pair, pair, pair, pl.BlockSpec((1, n_e), lambda i: (0, 0))),
        scratch_shapes=[pltpu.VMEM((1, n_e), F32)],
        compiler_params=_params(("arbitrary",)),
        name="moe_route",
    )(x2d, g, router_w, router_b)


def _wait_rows(hbm_ref, vmem_slot, sem, n_rows):
    pltpu.make_async_copy(hbm_ref.at[pl.ds(0, n_rows)], vmem_slot, sem).wait()


def _dispatch_kernel(pos_ref, x_ref, g_ref, zeros_ref, hs_ref, h_scr, sem):
    del zeros_ref
    i = pl.program_id(0)
    n = pl.num_programs(0)
    tm = x_ref.shape[0]
    slot = lax.rem(i, 2)

    def wait_slot(s):
        for _ in range(TOP_K):
            _wait_rows(hs_ref, h_scr.at[s], sem.at[s], tm)

    @pl.when(i >= 2)
    def _():
        wait_slot(slot)

    h_scr[slot] = _rmsnorm(x_ref[...], g_ref[...])
    base = i * (TOP_K * tm)

    def issue(r, carry):
        for k in range(TOP_K):
            p = pos_ref[base + TOP_K * r + k]
            pltpu.make_async_copy(h_scr.at[slot, pl.ds(r, 1)], hs_ref.at[pl.ds(p, 1)],
                                  sem.at[slot]).start()
        return carry

    lax.fori_loop(0, tm, issue, 0, unroll=8)

    @pl.when(i == n - 1)
    def _():
        wait_slot(slot)

    @pl.when((i == n - 1) & (n >= 2))
    def _():
        wait_slot(1 - slot)


def _dispatch(pos, x2d, g, n_rows, *, tm=256):
    t, d = x2d.shape
    zeros = jnp.zeros((n_rows, d), F32)
    return pl.pallas_call(
        _dispatch_kernel,
        out_shape=jax.ShapeDtypeStruct((n_rows, d), F32),
        grid_spec=pltpu.PrefetchScalarGridSpec(
            num_scalar_prefetch=1,
            grid=(t // tm,),
            in_specs=[pl.BlockSpec((tm, d), lambda i, pos: (i, 0)),
                      pl.BlockSpec((1, d), lambda i, pos: (0, 0)),
                      pl.BlockSpec(memory_space=pl.ANY)],
            out_specs=pl.BlockSpec(memory_space=pl.ANY),
            scratch_shapes=[pltpu.VMEM((2, tm, d), F32), pltpu.SemaphoreType.DMA((2,))],
        ),
        input_output_aliases={3: 0},
        compiler_params=_params(("arbitrary",)),
        name="moe_dispatch",
    )(pos, x2d, g, zeros)


def _experts_kernel(te_ref, nused_ref, hs_ref, wg_ref, wu_ref, wd_ref, y_ref, *, f_chunk):
    del te_ref
    i = pl.program_id(0)

    @pl.when(i < nused_ref[0])
    def _():
        h = hs_ref[...].astype(BF16)
        for c in range(0, wg_ref.shape[1], f_chunk):
            a = jnp.dot(h, wg_ref[:, c:c + f_chunk], preferred_element_type=F32)
            b = jnp.dot(h, wu_ref[:, c:c + f_chunk], preferred_element_type=F32)
            act = (jax.nn.silu(a) * b).astype(BF16)
            y = jnp.dot(act, wd_ref[c:c + f_chunk, :], preferred_element_type=F32)
            if c == 0:
                y_ref[...] = y
            else:
                y_ref[...] += y

    @pl.when(i >= nused_ref[0])
    def _():
        y_ref[...] = jnp.zeros_like(y_ref)


def _experts(tile_expert, n_used, hs, w_gate, w_up, w_down, *, tm, f_chunk=512):
    n_rows, d = hs.shape
    f_dim = w_gate.shape[2]

    def row_blk(i, te, nu):
        return (jnp.minimum(i, nu[0] - 1), 0)

    expert_blk = lambda i, te, nu: (te[i], 0, 0)
    return pl.pallas_call(
        functools.partial(_experts_kernel, f_chunk=f_chunk),
        out_shape=jax.ShapeDtypeStruct((n_rows, d), F32),
        grid_spec=pltpu.PrefetchScalarGridSpec(
            num_scalar_prefetch=2,
            grid=(n_rows // tm,),
            in_specs=[
                pl.BlockSpec((tm, d), row_blk),
                pl.BlockSpec((None, d, f_dim), expert_blk),
                pl.BlockSpec((None, d, f_dim), expert_blk),
                pl.BlockSpec((None, f_dim, d), expert_blk),
            ],
            out_specs=pl.BlockSpec((tm, d), lambda i, te, nu: (i, 0)),
        ),
        compiler_params=pltpu.CompilerParams(dimension_semantics=("arbitrary",),
                                             vmem_limit_bytes=EXPERTS_VMEM_LIMIT_BYTES),
        name="moe_experts",
    )(tile_expert, n_used, hs, w_gate, w_up, w_down)


def _combine_kernel(pos_ref, x_ref, w_ref, fg_ref, y_ref, o_ref, buf, sem):
    i = pl.program_id(0)
    n = pl.num_programs(0)
    tm = x_ref.shape[0]
    slot = lax.rem(i, 2)

    def gather(tile, s):
        base = tile * (TOP_K * tm)

        def issue(r, carry):
            for k in range(TOP_K):
                p = pos_ref[base + TOP_K * r + k]
                pltpu.make_async_copy(y_ref.at[pl.ds(p, 1)], buf.at[s, k, pl.ds(r, 1)],
                                      sem.at[s]).start()
            return carry

        lax.fori_loop(0, tm, issue, 0, unroll=8)

    @pl.when(i == 0)
    def _():
        gather(0, 0)

    @pl.when(i + 1 < n)
    def _():
        gather(i + 1, 1 - slot)

    for k in range(TOP_K):
        _wait_rows(y_ref, buf.at[slot, k], sem.at[slot], tm)

    w = w_ref[...]
    y = x_ref[...] + (w[:, 0:1] * buf[slot, 0] + w[:, 1:2] * buf[slot, 1])
    o_ref[...] = _rmsnorm(y, fg_ref[...])


def _combine(pos, x2d, w, final_g, y_sorted, *, tm=256):
    t, d = x2d.shape
    return pl.pallas_call(
        _combine_kernel,
        out_shape=jax.ShapeDtypeStruct((t, d), F32),
        grid_spec=pltpu.PrefetchScalarGridSpec(
            num_scalar_prefetch=1,
            grid=(t // tm,),
            in_specs=[pl.BlockSpec((tm, d), lambda i, pos: (i, 0)),
                      pl.BlockSpec((tm, TOP_K), lambda i, pos: (i, 0)),
                      pl.BlockSpec((1, d), lambda i, pos: (0, 0)),
                      pl.BlockSpec(memory_space=pl.ANY)],
            out_specs=pl.BlockSpec((tm, d), lambda i, pos: (i, 0)),
            scratch_shapes=[pltpu.VMEM((2, TOP_K, tm, d), F32), pltpu.SemaphoreType.DMA((2,))],
        ),
        compiler_params=_params(("arbitrary",)),
        name="moe_combine",
    )(pos, x2d, w, final_g, y_sorted)


def _moe_layer(x2d, g, router_w, router_b, w_gate, w_up, w_down, final_g, *, row_tile=512):
    t, d = x2d.shape
    n_e = router_w.shape[1]
    idx, rank, w, counts = _route(x2d, g, router_w, router_b)

    counts = counts[0]
    padded = (counts + row_tile - 1) // row_tile * row_tile
    ends = jnp.cumsum(padded)
    starts = ends - padded
    experts = jnp.arange(n_e, dtype=jnp.int32)
    pos = rank + jnp.sum(jnp.where(idx[..., None] == experts, starts, 0), axis=-1)
    pos = pos.reshape(-1).astype(jnp.int32)

    n_tiles = (TOP_K * t) // row_tile + n_e
    tile_ends = ends // row_tile
    n_used = tile_ends[-1:].astype(jnp.int32)
    tile_ids = jnp.arange(n_tiles, dtype=jnp.int32)
    tile_expert = jnp.sum(jnp.minimum(tile_ids, n_used - 1)[:, None] >= tile_ends[None, :],
                          axis=1).astype(jnp.int32)

    hs = _dispatch(pos, x2d, g, n_tiles * row_tile)
    y_sorted = _experts(tile_expert, n_used, hs, w_gate, w_up, w_down, tm=row_tile)
    return _combine(pos, x2d, w, final_g, y_sorted)


def kernel(x, norm_mix_g, norm_ffn_g, final_g, a_w_in, a_v_gain, a_w_s, a_b_s, a_w_out,
           a_ffn_gate, a_ffn_up, a_ffn_down, b_w_qkv, b_rel_bias, b_w_o, b_router_w,
           b_router_b, b_exp_gate, b_exp_up, b_exp_down):
    batch, seq, d = x.shape
    assert norm_mix_g.shape[0] == 2, "two layers: gMLP/SwiGLU then attention/MoE"
    head_dim = d // ATT_HEADS
    tq = 256
    xf = x.reshape(batch * seq, d)
    row = lambda vec: vec.reshape(1, -1).astype(F32)

    xf = _gmlp_layer(xf, row(norm_mix_g[0]), a_w_in[0].astype(BF16), row(a_v_gain[0]),
                     a_w_s[0], a_b_s[0].T, a_w_out[0].astype(BF16))
    xf = _swiglu_layer(xf, row(norm_ffn_g[0]), a_ffn_gate[0].astype(BF16),
                       a_ffn_up[0].astype(BF16), a_ffn_down[0].astype(BF16))

    q, k, v = _qkv_layer(xf, row(norm_mix_g[1]), b_w_qkv[0].astype(BF16),
                         q_scale=head_dim ** -0.5)
    bias = _band_bias(b_rel_bias[0], tq)
    xf = _attn_layer(xf, q, k, v, bias, b_w_o[0].astype(BF16), batch=batch, tq=tq)
    xf = _moe_layer(xf, row(norm_ffn_g[1]), b_router_w[0], row(b_router_b[0]),
                    b_exp_gate[0].astype(BF16), b_exp_up[0].astype(BF16),
                    b_exp_down[0].astype(BF16), row(final_g))
    return xf.reshape(batch, seq, d)
```

```python
import functools

import jax
import jax.numpy as jnp
from jax import lax
from jax.experimental import pallas as pl
from jax.experimental.pallas import tpu as pltpu

EPS = 1e-6
CHUNK = 64
GMLP_BLOCK = 128
GMLP_GROUPS = 8
ATT_HEADS = 16
ATT_LEFT_CHUNKS = 8
REL_CLIP = 256
TOP_K = 2
NEG_INF = -1e30

VMEM_LIMIT_BYTES = 56 * 1024 * 1024
EXPERTS_VMEM_LIMIT_BYTES = 60 * 1024 * 1024

F32 = jnp.float32
BF16 = jnp.bfloat16


def _params(semantics):
    return pltpu.CompilerParams(dimension_semantics=semantics,
                                vmem_limit_bytes=VMEM_LIMIT_BYTES)


def _resident(shape):
    zeros = (0,) * len(shape)
    return pl.BlockSpec(shape, lambda *_: zeros, pipeline_mode=pl.Buffered(1))


def _rmsnorm(x, g):
    ms = jnp.mean(x * x, axis=-1, keepdims=True)
    return x * lax.rsqrt(ms + EPS) * g


def _gelu_exact(z):
    return 0.5 * z * (1.0 + lax.erf(z * (2.0 ** -0.5)))


def _gmlp_kernel(x_ref, g_ref, win_ref, vgain_ref, ws_ref, bst_ref, wout_ref,
                 o_ref, u_scr, v_scr, gated_scr, *, n_chunk):
    tm = x_ref.shape[0]
    half = u_scr.shape[1]
    group_dim = half // GMLP_GROUPS
    x = x_ref[...]
    h = _rmsnorm(x, g_ref[...]).astype(BF16)

    ssq = jnp.zeros((tm, 1), F32)
    for c in range(0, 2 * half, n_chunk):
        z = jnp.dot(h, win_ref[:, c:c + n_chunk], preferred_element_type=F32)
        z = _gelu_exact(z)
        if c < half:
            u_scr[:, c:c + n_chunk] = z
        else:
            v_scr[:, c - half:c - half + n_chunk] = z
            ssq = ssq + jnp.sum(z * z, axis=-1, keepdims=True)
    inv = lax.rsqrt(ssq * (1.0 / half) + EPS)

    row_chunk = lax.broadcasted_iota(jnp.int32, (GMLP_BLOCK, GMLP_BLOCK), 0) // CHUNK
    col_chunk = lax.broadcasted_iota(jnp.int32, (GMLP_BLOCK, GMLP_BLOCK), 1) // CHUNK
    causal = row_chunk >= col_chunk
    for g in range(GMLP_GROUPS):
        cols = slice(g * group_dim, (g + 1) * group_dim)
        w_g = jnp.where(causal, ws_ref[g], 0.0).astype(BF16)
        b_g = bst_ref[:, g:g + 1]
        for n in range(tm // GMLP_BLOCK):
            rows = slice(n * GMLP_BLOCK, (n + 1) * GMLP_BLOCK)
            vn = (v_scr[rows, cols] * inv[rows] * vgain_ref[:, cols]).astype(BF16)
            sv = jnp.dot(w_g, vn, preferred_element_type=F32) + b_g
            gated_scr[rows, cols] = (u_scr[rows, cols] * sv).astype(BF16)

    out = jnp.dot(gated_scr[...], wout_ref[...], preferred_element_type=F32)
    o_ref[...] = x + out


def _gmlp_layer(x2d, g, w_in, v_gain, w_s, b_s_t, w_out, *, tm=512, n_chunk=512):
    t, d = x2d.shape
    half = w_out.shape[0]
    return pl.pallas_call(
        functools.partial(_gmlp_kernel, n_chunk=n_chunk),
        out_shape=jax.ShapeDtypeStruct((t, d), F32),
        grid=(t // tm,),
        in_specs=[
            pl.BlockSpec((tm, d), lambda i: (i, 0)),
            _resident((1, d)),
            _resident(w_in.shape),
            _resident((1, half)),
            _resident(w_s.shape),
            _resident(b_s_t.shape),
            _resident(w_out.shape),
        ],
        out_specs=pl.BlockSpec((tm, d), lambda i: (i, 0)),
        scratch_shapes=[
            pltpu.VMEM((tm, half), F32),
            pltpu.VMEM((tm, half), F32),
            pltpu.VMEM((tm, half), BF16),
        ],
        compiler_params=_params(("parallel",)),
        name="gmlp_mixer",
    )(x2d, g, w_in, v_gain, w_s, b_s_t, w_out)


def _swiglu_kernel(x_ref, g_ref, wg_ref, wu_ref, wd_ref, o_ref, *, f_chunk):
    x = x_ref[...]
    h = _rmsnorm(x, g_ref[...]).astype(BF16)
    o_ref[...] = x
    for c in range(0, wg_ref.shape[1], f_chunk):
        a = jnp.dot(h, wg_ref[:, c:c + f_chunk], preferred_element_type=F32)
        b = jnp.dot(h, wu_ref[:, c:c + f_chunk], preferred_element_type=F32)
        act = (jax.nn.silu(a) * b).astype(BF16)
        o_ref[...] += jnp.dot(act, wd_ref[c:c + f_chunk, :], preferred_element_type=F32)


def _swiglu_layer(x2d, g, w_gate, w_up, w_down, *, tm=512, f_chunk=256):
    t, d = x2d.shape
    return pl.pallas_call(
        functools.partial(_swiglu_kernel, f_chunk=f_chunk),
        out_shape=jax.ShapeDtypeStruct((t, d), F32),
        grid=(t // tm,),
        in_specs=[
            pl.BlockSpec((tm, d), lambda i: (i, 0)),
            _resident((1, d)),
            _resident(w_gate.shape),
            _resident(w_up.shape),
            _resident(w_down.shape),
        ],
        out_specs=pl.BlockSpec((tm, d), lambda i: (i, 0)),
        compiler_params=_params(("parallel",)),
        name="dense_swiglu",
    )(x2d, g, w_gate, w_up, w_down)


def _qkv_kernel(x_ref, g_ref, wqt_ref, wk_ref, wvt_ref, qt_ref, k_ref, vt_ref, *, q_scale):
    h = _rmsnorm(x_ref[...], g_ref[...]).astype(BF16)
    nt_dims = (((1,), (1,)), ((), ()))
    qt = lax.dot_general(wqt_ref[...], h, nt_dims, preferred_element_type=F32)
    qt_ref[...] = (qt * q_scale).astype(BF16)
    k_ref[...] = jnp.dot(h, wk_ref[...], preferred_element_type=F32).astype(BF16)
    vt_ref[...] = lax.dot_general(wvt_ref[...], h, nt_dims,
                                  preferred_element_type=F32).astype(BF16)


def _qkv_layer(x2d, g, w_q_t, w_k, w_v_t, *, q_scale, tm=512):
    t, d = x2d.shape
    row_spec = pl.BlockSpec((tm, d), lambda i: (i, 0))
    col_spec = pl.BlockSpec((d, tm), lambda i: (0, i))
    return pl.pallas_call(
        functools.partial(_qkv_kernel, q_scale=q_scale),
        out_shape=(jax.ShapeDtypeStruct((d, t), BF16), jax.ShapeDtypeStruct((t, d), BF16),
                   jax.ShapeDtypeStruct((d, t), BF16)),
        grid=(t // tm,),
        in_specs=[row_spec, _resident((1, d)), _resident((d, d)), _resident((d, d)),
                  _resident((d, d))],
        out_specs=(col_spec, row_spec, col_spec),
        compiler_params=_params(("parallel",)),
        name="qkv_proj",
    )(x2d, g, w_q_t, w_k, w_v_t)


def _attn_kernel(x_ref, qt_ref, k0_ref, k1_ref, k2_ref, vt0_ref, vt1_ref, vt2_ref,
                 bias_ref, wot_ref, o_ref, ctx_scr, *, head_dim):
    tq = qt_ref.shape[1]
    i = pl.program_id(1)
    k_refs = (k0_ref, k1_ref, k2_ref)
    vt_refs = (vt0_ref, vt1_ref, vt2_ref)
    n_win = len(k_refs) * tq
    half_q = tq // 2
    assert (ATT_LEFT_CHUNKS + 1) * CHUNK + half_q - CHUNK <= n_win - half_q
    key_pos = lax.broadcasted_iota(jnp.int32, (n_win, 1), 0) + (i - 2) * tq
    in_seq = key_pos >= 0
    zero_half = jnp.zeros((head_dim, tq), BF16)

    def scores(h):
        pair = slice((h // 2) * 2 * head_dim, (h // 2 + 1) * 2 * head_dim)
        qt_h = qt_ref[h * head_dim:(h + 1) * head_dim, :]
        rhs = jnp.concatenate([qt_h, zero_half] if h % 2 == 0 else [zero_half, qt_h], axis=0)
        return jnp.concatenate(
            [jnp.dot(kr[:, pair], rhs, preferred_element_type=F32) for kr in k_refs],
            axis=0)

    def all_heads(window_has_padding):
        s_next = scores(0)
        for h in range(ATT_HEADS):
            rows = slice(h * head_dim, (h + 1) * head_dim)
            s = s_next
            if h + 1 < ATT_HEADS:
                s_next = scores(h + 1)
            halves = []
            for lanes, keys in ((slice(0, half_q), slice(0, n_win - half_q)),
                                (slice(half_q, tq), slice(half_q, n_win))):
                sh = s[keys, lanes] + bias_ref[h, keys, lanes]
                if window_has_padding:
                    sh = jnp.where(in_seq[keys], sh, NEG_INF)
                m = jnp.max(sh, axis=0, keepdims=True)
                p = jnp.exp(sh - m)
                halves.append((jnp.sum(p, axis=0, keepdims=True), p.astype(BF16)))
            unseen = jnp.zeros((half_q, half_q), BF16)
            l = jnp.concatenate([halves[0][0], halves[1][0]], axis=1)
            pb = jnp.concatenate([jnp.concatenate([halves[0][1], unseen], axis=0),
                                  jnp.concatenate([unseen, halves[1][1]], axis=0)], axis=1)
            ctx = jnp.zeros((head_dim, tq), F32)
            for j, vr in enumerate(vt_refs):
                ctx = ctx + jnp.dot(vr[rows, :], pb[j * tq:(j + 1) * tq, :],
                                    preferred_element_type=F32)
            ctx_scr[rows, :] = (ctx / l).astype(BF16)
        out_t = jnp.dot(wot_ref[...], ctx_scr[...], preferred_element_type=F32)
        o_ref[...] = x_ref[...] + out_t.T

    pl.when(i < 2)(functools.partial(all_heads, True))
    pl.when(i >= 2)(functools.partial(all_heads, False))


def _attn_layer(x2d, q_t, k, v_t, bias_t, w_o_t, *, batch, tq=256):
    t, d = x2d.shape
    nt = t // batch // tq

    def cur(b, i):
        return b * nt + i

    def back(n):
        return lambda b, i: b * nt + jnp.maximum(i - n, 0)

    rows = lambda f: pl.BlockSpec((tq, d), lambda b, i: (f(b, i), 0))
    cols = lambda f: pl.BlockSpec((d, tq), lambda b, i: (0, f(b, i)))
    return pl.pallas_call(
        functools.partial(_attn_kernel, head_dim=d // ATT_HEADS),
        out_shape=jax.ShapeDtypeStruct((t, d), F32),
        grid=(batch, nt),
        in_specs=[rows(cur), cols(cur),
                  rows(back(2)), rows(back(1)), rows(cur),
                  cols(back(2)), cols(back(1)), cols(cur),
                  _resident(bias_t.shape), _resident(w_o_t.shape)],
        out_specs=rows(cur),
        scratch_shapes=[pltpu.VMEM((d, tq), BF16)],
        compiler_params=_params(("parallel", "arbitrary")),
        name="band_attention",
    )(x2d, q_t, k, k, k, v_t, v_t, v_t, bias_t, w_o_t)


def _band_bias(rel_table, tq):
    band = (ATT_LEFT_CHUNKS + 1) * CHUNK
    n_heads = rel_table.shape[0]
    assert 3 * tq >= band + tq - CHUNK and rel_table.shape[1] == REL_CLIP + CHUNK
    tail = jnp.broadcast_to(rel_table[:, -1:], (n_heads, band - 1 - REL_CLIP))
    rev = jnp.concatenate([rel_table, tail], axis=1).astype(F32)[:, ::-1]
    chunk = jnp.stack([rev[:, CHUNK - 1 - qi:CHUNK - 1 - qi + band] for qi in range(CHUNK)],
                      axis=1)
    tiles = [jnp.pad(chunk, ((0, 0), (0, 0), (c * CHUNK, 3 * tq - band - c * CHUNK)),
                     constant_values=NEG_INF) for c in range(tq // CHUNK)]
    return jnp.swapaxes(jnp.concatenate(tiles, axis=1), 1, 2)


def _route_kernel(x_ref, g_ref, rw_ref, rb_ref, idx_ref, rank_ref, w_ref, cnt_ref, run_scr):
    tm = x_ref.shape[0]

    @pl.when(pl.program_id(0) == 0)
    def _():
        run_scr[...] = jnp.zeros_like(run_scr)

    hf = _rmsnorm(x_ref[...], g_ref[...])
    logits = jnp.dot(hf, rw_ref[...], preferred_element_type=F32,
                     precision=lax.Precision.HIGHEST) + rb_ref[...]
    n_e = logits.shape[-1]
    lane = lax.broadcasted_iota(jnp.int32, logits.shape, 1)
    m1 = jnp.max(logits, axis=-1, keepdims=True)
    i1 = jnp.min(jnp.where(logits == m1, lane, n_e), axis=-1, keepdims=True)
    rest = jnp.where(lane == i1, -jnp.inf, logits)
    m2 = jnp.max(rest, axis=-1, keepdims=True)
    i2 = jnp.min(jnp.where(rest == m2, lane, n_e), axis=-1, keepdims=True)
    e2 = jnp.exp(m2 - m1)
    denom = 1.0 + e2

    sel = jnp.where((lane == i1) | (lane == i2), 1.0, 0.0)
    before = (lax.broadcasted_iota(jnp.int32, (tm, tm), 0)
              > lax.broadcasted_iota(jnp.int32, (tm, tm), 1))
    cum = jnp.dot(jnp.where(before, 1.0, 0.0).astype(BF16), sel.astype(BF16),
                  preferred_element_type=F32) + run_scr[...]
    r1 = jnp.sum(jnp.where(lane == i1, cum, 0.0), axis=-1, keepdims=True)
    r2 = jnp.sum(jnp.where(lane == i2, cum, 0.0), axis=-1, keepdims=True)
    run_scr[...] += jnp.sum(sel, axis=0, keepdims=True)

    idx_ref[...] = jnp.concatenate([i1, i2], axis=1)
    rank_ref[...] = jnp.concatenate([r1, r2], axis=1).astype(jnp.int32)
    w_ref[...] = jnp.concatenate([1.0 / denom, e2 / denom], axis=1)
    cnt_ref[...] = run_scr[...].astype(jnp.int32)


def _route(x2d, g, router_w, router_b, *, tm=512):
    t, d = x2d.shape
    n_e = router_w.shape[1]
    pair = pl.BlockSpec((tm, TOP_K), lambda i: (i, 0))
    return pl.pallas_call(
        _route_kernel,
        out_shape=(jax.ShapeDtypeStruct((t, TOP_K), jnp.int32),
                   jax.ShapeDtypeStruct((t, TOP_K), jnp.int32),
                   jax.ShapeDtypeStruct((t, TOP_K), F32),
                   jax.ShapeDtypeStruct((1, n_e), jnp.int32)),
        grid=(t // tm,),
        in_specs=[pl.BlockSpec((tm, d), lambda i: (i, 0)), _resident((1, d)),
                  _resident(router_w.shape), _resident((1, n_e))],
        out_specs=(pair, pair, pair, pl.BlockSpec((1, n_e), lambda i: (0, 0))),
        scratch_shapes=[pltpu.VMEM((1, n_e), F32)],
        compiler_params=_params(("arbitrary",)),
        name="moe_route",
    )(x2d, g, router_w, router_b)


def _wait_rows(hbm_ref, vmem_slot, sem, n_rows):
    pltpu.make_async_copy(hbm_ref.at[pl.ds(0, n_rows)], vmem_slot, sem).wait()


def _dispatch_kernel(pos_ref, x_ref, g_ref, zeros_ref, hs_ref, h_scr, sem):
    del zeros_ref
    i = pl.program_id(0)
    n = pl.num_programs(0)
    tm = x_ref.shape[0]
    slot = lax.rem(i, 2)

    def wait_slot(s):
        for _ in range(TOP_K):
            _wait_rows(hs_ref, h_scr.at[s], sem.at[s], tm)

    @pl.when(i >= 2)
    def _():
        wait_slot(slot)

    h_scr[slot] = _rmsnorm(x_ref[...], g_ref[...])
    base = i * (TOP_K * tm)

    def issue(r, carry):
        for k in range(TOP_K):
            p = pos_ref[base + TOP_K * r + k]
            pltpu.make_async_copy(h_scr.at[slot, pl.ds(r, 1)], hs_ref.at[pl.ds(p, 1)],
                                  sem.at[slot]).start()
        return carry

    lax.fori_loop(0, tm, issue, 0, unroll=8)

    @pl.when(i == n - 1)
    def _():
        wait_slot(slot)

    @pl.when((i == n - 1) & (n >= 2))
    def _():
        wait_slot(1 - slot)


def _dispatch(pos, x2d, g, n_rows, *, tm=256):
    t, d = x2d.shape
    zeros = jnp.zeros((n_rows, d), F32)
    return pl.pallas_call(
        _dispatch_kernel,
        out_shape=jax.ShapeDtypeStruct((n_rows, d), F32),
        grid_spec=pltpu.PrefetchScalarGridSpec(
            num_scalar_prefetch=1,
            grid=(t // tm,),
            in_specs=[pl.BlockSpec((tm, d), lambda i, pos: (i, 0)),
                      pl.BlockSpec((1, d), lambda i, pos: (0, 0)),
                      pl.BlockSpec(memory_space=pl.ANY)],
            out_specs=pl.BlockSpec(memory_space=pl.ANY),
            scratch_shapes=[pltpu.VMEM((2, tm, d), F32), pltpu.SemaphoreType.DMA((2,))],
        ),
        input_output_aliases={3: 0},
        compiler_params=_params(("arbitrary",)),
        name="moe_dispatch",
    )(pos, x2d, g, zeros)


def _experts_kernel(te_ref, nused_ref, hs_ref, wg_ref, wu_ref, wd_ref, y_ref, *, f_chunk):
    del te_ref
    i = pl.program_id(0)

    @pl.when(i < nused_ref[0])
    def _():
        h = hs_ref[...].astype(BF16)
        for c in range(0, wg_ref.shape[1], f_chunk):
            a = jnp.dot(h, wg_ref[:, c:c + f_chunk], preferred_element_type=F32)
            b = jnp.dot(h, wu_ref[:, c:c + f_chunk], preferred_element_type=F32)
            act = (jax.nn.silu(a) * b).astype(BF16)
            y = jnp.dot(act, wd_ref[c:c + f_chunk, :], preferred_element_type=F32)
            if c == 0:
                y_ref[...] = y
            else:
                y_ref[...] += y

    @pl.when(i >= nused_ref[0])
    def _():
        y_ref[...] = jnp.zeros_like(y_ref)


def _experts(tile_expert, n_used, hs, w_gate, w_up, w_down, *, tm, f_chunk=512):
    n_rows, d = hs.shape
    f_dim = w_gate.shape[2]

    def row_blk(i, te, nu):
        return (jnp.maximum(jnp.minimum(i, nu[0] - 1), 0), 0)

    expert_blk = lambda i, te, nu: (te[i], 0, 0)
    return pl.pallas_call(
        functools.partial(_experts_kernel, f_chunk=f_chunk),
        out_shape=jax.ShapeDtypeStruct((n_rows, d), F32),
        grid_spec=pltpu.PrefetchScalarGridSpec(
            num_scalar_prefetch=2,
            grid=(n_rows // tm,),
            in_specs=[
                pl.BlockSpec((tm, d), row_blk),
                pl.BlockSpec((None, d, f_dim), expert_blk),
                pl.BlockSpec((None, d, f_dim), expert_blk),
                pl.BlockSpec((None, f_dim, d), expert_blk),
            ],
            out_specs=pl.BlockSpec((tm, d), lambda i, te, nu: (i, 0)),
        ),
        compiler_params=pltpu.CompilerParams(dimension_semantics=("arbitrary",),
                                             vmem_limit_bytes=EXPERTS_VMEM_LIMIT_BYTES),
        name="moe_experts",
    )(tile_expert, n_used, hs, w_gate, w_up, w_down)


def _combine_kernel(pos_ref, x_ref, w_ref, fg_ref, y_ref, o_ref, buf, sem):
    i = pl.program_id(0)
    n = pl.num_programs(0)
    tm = x_ref.shape[0]
    slot = lax.rem(i, 2)

    def gather(tile, s):
        base = tile * (TOP_K * tm)

        def issue(r, carry):
            for k in range(TOP_K):
                p = pos_ref[base + TOP_K * r + k]
                pltpu.make_async_copy(y_ref.at[pl.ds(p, 1)], buf.at[s, k, pl.ds(r, 1)],
                                      sem.at[s]).start()
            return carry

        lax.fori_loop(0, tm, issue, 0, unroll=8)

    @pl.when(i == 0)
    def _():
        gather(0, 0)

    @pl.when(i + 1 < n)
    def _():
        gather(i + 1, 1 - slot)

    for k in range(TOP_K):
        _wait_rows(y_ref, buf.at[slot, k], sem.at[slot], tm)

    w = w_ref[...]
    y = x_ref[...] + (w[:, 0:1] * buf[slot, 0] + w[:, 1:2] * buf[slot, 1])
    o_ref[...] = _rmsnorm(y, fg_ref[...])


def _combine(pos, x2d, w, final_g, y_sorted, *, tm=256):
    t, d = x2d.shape
    return pl.pallas_call(
        _combine_kernel,
        out_shape=jax.ShapeDtypeStruct((t, d), F32),
        grid_spec=pltpu.PrefetchScalarGridSpec(
            num_scalar_prefetch=1,
            grid=(t // tm,),
            in_specs=[pl.BlockSpec((tm, d), lambda i, pos: (i, 0)),
                      pl.BlockSpec((tm, TOP_K), lambda i, pos: (i, 0)),
                      pl.BlockSpec((1, d), lambda i, pos: (0, 0)),
                      pl.BlockSpec(memory_space=pl.ANY)],
            out_specs=pl.BlockSpec((tm, d), lambda i, pos: (i, 0)),
            scratch_shapes=[pltpu.VMEM((2, TOP_K, tm, d), F32), pltpu.SemaphoreType.DMA((2,))],
        ),
        compiler_params=_params(("arbitrary",)),
        name="moe_combine",
    )(pos, x2d, w, final_g, y_sorted)


def _moe_layer(x2d, g, router_w, router_b, w_gate, w_up, w_down, final_g, *, row_tile=512):
    t, d = x2d.shape
    n_e = router_w.shape[1]
    idx, rank, w, counts = _route(x2d, g, router_w, router_b)

    counts = counts[0]
    padded = (counts + row_tile - 1) // row_tile * row_tile
    ends = jnp.cumsum(padded)
    starts = ends - padded
    experts = jnp.arange(n_e, dtype=jnp.int32)
    pos = rank + jnp.sum(jnp.where(idx[..., None] == experts, starts, 0), axis=-1)
    pos = pos.reshape(-1).astype(jnp.int32)

    n_tiles = (TOP_K * t) // row_tile + n_e
    tile_ends = ends // row_tile
    n_used = tile_ends[-1:].astype(jnp.int32)
    tile_ids = jnp.arange(n_tiles, dtype=jnp.int32)
    tile_expert = jnp.sum(jnp.minimum(tile_ids, n_used - 1)[:, None] >= tile_ends[None, :],
                          axis=1).astype(jnp.int32)

    hs = _dispatch(pos, x2d, g, n_tiles * row_tile)
    y_sorted = _experts(tile_expert, n_used, hs, w_gate, w_up, w_down, tm=row_tile)
    return _combine(pos, x2d, w, final_g, y_sorted)


def kernel(x, norm_mix_g, norm_ffn_g, final_g, a_w_in, a_v_gain, a_w_s, a_b_s, a_w_out,
           a_ffn_gate, a_ffn_up, a_ffn_down, b_w_qkv, b_rel_bias, b_w_o, b_router_w,
           b_router_b, b_exp_gate, b_exp_up, b_exp_down):
    batch, seq, d = x.shape
    assert norm_mix_g.shape[0] == 2, "two layers: gMLP/SwiGLU then attention/MoE"
    head_dim = d // ATT_HEADS
    tq = 256
    xf = x.reshape(batch * seq, d)
    row = lambda vec: vec.reshape(1, -1).astype(F32)

    xf = _gmlp_layer(xf, row(norm_mix_g[0]), a_w_in[0].astype(BF16), row(a_v_gain[0]),
                     a_w_s[0], a_b_s[0].T, a_w_out[0].astype(BF16))
    xf = _swiglu_layer(xf, row(norm_ffn_g[0]), a_ffn_gate[0].astype(BF16),
                       a_ffn_up[0].astype(BF16), a_ffn_down[0].astype(BF16))

    w_qkv = b_w_qkv[0].astype(BF16)
    q_t, k, v_t = _qkv_layer(xf, row(norm_mix_g[1]), w_qkv[:, :d].T, w_qkv[:, d:2 * d],
                             w_qkv[:, 2 * d:].T, q_scale=head_dim ** -0.5)
    bias_t = _band_bias(b_rel_bias[0], tq)
    xf = _attn_layer(xf, q_t, k, v_t, bias_t, b_w_o[0].astype(BF16).T, batch=batch, tq=tq)
    xf = _moe_layer(xf, row(norm_ffn_g[1]), b_router_w[0], row(b_router_b[0]),
                    b_exp_gate[0].astype(BF16), b_exp_up[0].astype(BF16),
                    b_exp_down[0].astype(BF16), row(final_g))
    return xf.reshape(batch, seq, d)
```

```python
import functools

import jax
import jax.numpy as jnp
from jax import lax
from jax.experimental import pallas as pl
from jax.experimental.pallas import tpu as pltpu

EPS = 1e-6
CHUNK = 64
GMLP_BLOCK = 128
GMLP_GROUPS = 8
ATT_HEADS = 16
ATT_LEFT_CHUNKS = 8
REL_CLIP = 256
TOP_K = 2
NEG_INF = -1e30

VMEM_LIMIT_BYTES = 56 * 1024 * 1024
EXPERTS_VMEM_LIMIT_BYTES = 60 * 1024 * 1024

F32 = jnp.float32
BF16 = jnp.bfloat16


def _params(semantics):
    return pltpu.CompilerParams(dimension_semantics=semantics,
                                vmem_limit_bytes=VMEM_LIMIT_BYTES)


def _resident(shape):
    zeros = (0,) * len(shape)
    return pl.BlockSpec(shape, lambda *_: zeros, pipeline_mode=pl.Buffered(1))


def _rmsnorm(x, g):
    ms = jnp.mean(x * x, axis=-1, keepdims=True)
    return x * lax.rsqrt(ms + EPS) * g


def _gelu_exact(z):
    return 0.5 * z * (1.0 + lax.erf(z * (2.0 ** -0.5)))


def _gmlp_kernel(x_ref, g_ref, win_ref, vgain_ref, ws_ref, bst_ref, wout_ref,
                 o_ref, u_scr, v_scr, gated_scr, *, n_chunk):
    tm = x_ref.shape[0]
    half = u_scr.shape[1]
    group_dim = half // GMLP_GROUPS
    x = x_ref[...]
    h = _rmsnorm(x, g_ref[...]).astype(BF16)

    ssq = jnp.zeros((tm, 1), F32)
    for c in range(0, 2 * half, n_chunk):
        z = jnp.dot(h, win_ref[:, c:c + n_chunk], preferred_element_type=F32)
        z = _gelu_exact(z)
        if c < half:
            u_scr[:, c:c + n_chunk] = z
        else:
            v_scr[:, c - half:c - half + n_chunk] = z
            ssq = ssq + jnp.sum(z * z, axis=-1, keepdims=True)
    inv = lax.rsqrt(ssq * (1.0 / half) + EPS)

    row_chunk = lax.broadcasted_iota(jnp.int32, (GMLP_BLOCK, GMLP_BLOCK), 0) // CHUNK
    col_chunk = lax.broadcasted_iota(jnp.int32, (GMLP_BLOCK, GMLP_BLOCK), 1) // CHUNK
    causal = row_chunk >= col_chunk
    for g in range(GMLP_GROUPS):
        cols = slice(g * group_dim, (g + 1) * group_dim)
        w_g = jnp.where(causal, ws_ref[g], 0.0).astype(BF16)
        b_g = bst_ref[:, g:g + 1]
        for n in range(tm // GMLP_BLOCK):
            rows = slice(n * GMLP_BLOCK, (n + 1) * GMLP_BLOCK)
            vn = (v_scr[rows, cols] * inv[rows] * vgain_ref[:, cols]).astype(BF16)
            sv = jnp.dot(w_g, vn, preferred_element_type=F32) + b_g
            gated_scr[rows, cols] = (u_scr[rows, cols] * sv).astype(BF16)

    out = jnp.dot(gated_scr[...], wout_ref[...], preferred_element_type=F32)
    o_ref[...] = x + out


def _gmlp_layer(x2d, g, w_in, v_gain, w_s, b_s_t, w_out, *, tm=512, n_chunk=512):
    t, d = x2d.shape
    half = w_out.shape[0]
    return pl.pallas_call(
        functools.partial(_gmlp_kernel, n_chunk=n_chunk),
        out_shape=jax.ShapeDtypeStruct((t, d), F32),
        grid=(t // tm,),
        in_specs=[
            pl.BlockSpec((tm, d), lambda i: (i, 0)),
            _resident((1, d)),
            _resident(w_in.shape),
            _resident((1, half)),
            _resident(w_s.shape),
            _resident(b_s_t.shape),
            _resident(w_out.shape),
        ],
        out_specs=pl.BlockSpec((tm, d), lambda i: (i, 0)),
        scratch_shapes=[
            pltpu.VMEM((tm, half), F32),
            pltpu.VMEM((tm, half), F32),
            pltpu.VMEM((tm, half), BF16),
        ],
        compiler_params=_params(("parallel",)),
        name="gmlp_mixer",
    )(x2d, g, w_in, v_gain, w_s, b_s_t, w_out)


def _swiglu_kernel(x_ref, g_ref, wg_ref, wu_ref, wd_ref, o_ref, *, f_chunk):
    x = x_ref[...]
    h = _rmsnorm(x, g_ref[...]).astype(BF16)
    o_ref[...] = x
    for c in range(0, wg_ref.shape[1], f_chunk):
        a = jnp.dot(h, wg_ref[:, c:c + f_chunk], preferred_element_type=F32)
        b = jnp.dot(h, wu_ref[:, c:c + f_chunk], preferred_element_type=F32)
        act = (jax.nn.silu(a) * b).astype(BF16)
        o_ref[...] += jnp.dot(act, wd_ref[c:c + f_chunk, :], preferred_element_type=F32)


def _swiglu_layer(x2d, g, w_gate, w_up, w_down, *, tm=512, f_chunk=256):
    t, d = x2d.shape
    return pl.pallas_call(
        functools.partial(_swiglu_kernel, f_chunk=f_chunk),
        out_shape=jax.ShapeDtypeStruct((t, d), F32),
        grid=(t // tm,),
        in_specs=[
            pl.BlockSpec((tm, d), lambda i: (i, 0)),
            _resident((1, d)),
            _resident(w_gate.shape),
            _resident(w_up.shape),
            _resident(w_down.shape),
        ],
        out_specs=pl.BlockSpec((tm, d), lambda i: (i, 0)),
        compiler_params=_params(("parallel",)),
        name="dense_swiglu",
    )(x2d, g, w_gate, w_up, w_down)


def _qkv_kernel(x_ref, g_ref, wqt_ref, wk_ref, wvt_ref, qt_ref, k_ref, vt_ref, *, q_scale):
    h = _rmsnorm(x_ref[...], g_ref[...]).astype(BF16)
    nt_dims = (((1,), (1,)), ((), ()))
    qt = lax.dot_general(wqt_ref[...], h, nt_dims, preferred_element_type=F32)
    qt_ref[...] = (qt * q_scale).astype(BF16)
    k_ref[...] = jnp.dot(h, wk_ref[...], preferred_element_type=F32).astype(BF16)
    vt_ref[...] = lax.dot_general(wvt_ref[...], h, nt_dims,
                                  preferred_element_type=F32).astype(BF16)


def _qkv_layer(x2d, g, w_q_t, w_k, w_v_t, *, q_scale, tm=512):
    t, d = x2d.shape
    row_spec = pl.BlockSpec((tm, d), lambda i: (i, 0))
    col_spec = pl.BlockSpec((d, tm), lambda i: (0, i))
    return pl.pallas_call(
        functools.partial(_qkv_kernel, q_scale=q_scale),
        out_shape=(jax.ShapeDtypeStruct((d, t), BF16), jax.ShapeDtypeStruct((t, d), BF16),
                   jax.ShapeDtypeStruct((d, t), BF16)),
        grid=(t // tm,),
        in_specs=[row_spec, _resident((1, d)), _resident((d, d)), _resident((d, d)),
                  _resident((d, d))],
        out_specs=(col_spec, row_spec, col_spec),
        compiler_params=_params(("parallel",)),
        name="qkv_proj",
    )(x2d, g, w_q_t, w_k, w_v_t)


def _attn_kernel(x_ref, qt_ref, k0_ref, k1_ref, k2_ref, vt0_ref, vt1_ref, vt2_ref,
                 bias_ref, wot_ref, o_ref, ctx_scr, *, head_dim):
    tq = qt_ref.shape[1]
    i = pl.program_id(1)
    k_refs = (k0_ref, k1_ref, k2_ref)
    vt_refs = (vt0_ref, vt1_ref, vt2_ref)
    n_win = len(k_refs) * tq
    half_q = tq // 2
    assert (ATT_LEFT_CHUNKS + 1) * CHUNK + half_q - CHUNK <= n_win - half_q
    key_pos = lax.broadcasted_iota(jnp.int32, (n_win, 1), 0) + (i - 2) * tq
    in_seq = key_pos >= 0
    zero_half = jnp.zeros((head_dim, tq), BF16)

    def scores(h):
        pair = slice((h // 2) * 2 * head_dim, (h // 2 + 1) * 2 * head_dim)
        qt_h = qt_ref[h * head_dim:(h + 1) * head_dim, :]
        rhs = jnp.concatenate([qt_h, zero_half] if h % 2 == 0 else [zero_half, qt_h], axis=0)
        return jnp.concatenate(
            [jnp.dot(kr[:, pair], rhs, preferred_element_type=F32) for kr in k_refs],
            axis=0)

    def all_heads(window_has_padding):
        s_next = scores(0)
        for h in range(ATT_HEADS):
            rows = slice(h * head_dim, (h + 1) * head_dim)
            s = s_next
            if h + 1 < ATT_HEADS:
                s_next = scores(h + 1)
            halves = []
            for lanes, keys in ((slice(0, half_q), slice(0, n_win - half_q)),
                                (slice(half_q, tq), slice(half_q, n_win))):
                sh = s[keys, lanes] + bias_ref[h, keys, lanes]
                if window_has_padding:
                    sh = jnp.where(in_seq[keys], sh, NEG_INF)
                m = jnp.max(sh, axis=0, keepdims=True)
                p = jnp.exp(sh - m)
                halves.append((jnp.sum(p, axis=0, keepdims=True), p.astype(BF16)))
            unseen = jnp.zeros((half_q, half_q), BF16)
            l = jnp.concatenate([halves[0][0], halves[1][0]], axis=1)
            pb = jnp.concatenate([jnp.concatenate([halves[0][1], unseen], axis=0),
                                  jnp.concatenate([unseen, halves[1][1]], axis=0)], axis=1)
            ctx = jnp.zeros((head_dim, tq), F32)
            for j, vr in enumerate(vt_refs):
                ctx = ctx + jnp.dot(vr[rows, :], pb[j * tq:(j + 1) * tq, :],
                                    preferred_element_type=F32)
            ctx_scr[rows, :] = (ctx / l).astype(BF16)
        out_t = jnp.dot(wot_ref[...], ctx_scr[...], preferred_element_type=F32)
        o_ref[...] = x_ref[...] + out_t.T

    pl.when(i < 2)(functools.partial(all_heads, True))
    pl.when(i >= 2)(functools.partial(all_heads, False))


def _attn_layer(x2d, q_t, k, v_t, bias_t, w_o_t, *, batch, tq=256):
    t, d = x2d.shape
    nt = t // batch // tq

    def cur(b, i):
        return b * nt + i

    def back(n):
        return lambda b, i: b * nt + jnp.maximum(i - n, 0)

    rows = lambda f: pl.BlockSpec((tq, d), lambda b, i: (f(b, i), 0))
    cols = lambda f: pl.BlockSpec((d, tq), lambda b, i: (0, f(b, i)))
    return pl.pallas_call(
        functools.partial(_attn_kernel, head_dim=d // ATT_HEADS),
        out_shape=jax.ShapeDtypeStruct((t, d), F32),
        grid=(batch, nt),
        in_specs=[rows(cur), cols(cur),
                  rows(back(2)), rows(back(1)), rows(cur),
                  cols(back(2)), cols(back(1)), cols(cur),
                  _resident(bias_t.shape), _resident(w_o_t.shape)],
        out_specs=rows(cur),
        scratch_shapes=[pltpu.VMEM((d, tq), BF16)],
        compiler_params=_params(("parallel", "arbitrary")),
        name="band_attention",
    )(x2d, q_t, k, k, k, v_t, v_t, v_t, bias_t, w_o_t)


def _band_bias(rel_table, tq):
    band = (ATT_LEFT_CHUNKS + 1) * CHUNK
    n_heads = rel_table.shape[0]
    assert 3 * tq >= band + tq - CHUNK and rel_table.shape[1] == REL_CLIP + CHUNK
    tail = jnp.broadcast_to(rel_table[:, -1:], (n_heads, band - 1 - REL_CLIP))
    ext = jnp.concatenate([rel_table, tail], axis=1).astype(F32)
    n_ext = band + CHUNK - 1
    start = jnp.concatenate([ext[:, band - 1:], ext[:, :band - 1]], axis=1)
    flat = jnp.broadcast_to(start[:, None, :], (n_heads, band, n_ext)).reshape(n_heads, -1)
    chunk_t = flat[:, :band * (n_ext - 1)].reshape(n_heads, band, n_ext - 1)[:, :, :CHUNK]
    tiles = [jnp.pad(chunk_t, ((0, 0), (c * CHUNK, 3 * tq - band - c * CHUNK), (0, 0)),
                     constant_values=NEG_INF) for c in range(tq // CHUNK)]
    return jnp.concatenate(tiles, axis=2)


def _route_kernel(x_ref, g_ref, rw_ref, rb_ref, idx_ref, rank_ref, w_ref, cnt_ref, run_scr):
    tm = x_ref.shape[0]

    @pl.when(pl.program_id(0) == 0)
    def _():
        run_scr[...] = jnp.zeros_like(run_scr)

    hf = _rmsnorm(x_ref[...], g_ref[...])
    logits = jnp.dot(hf, rw_ref[...], preferred_element_type=F32,
                     precision=lax.Precision.HIGHEST) + rb_ref[...]
    n_e = logits.shape[-1]
    lane = lax.broadcasted_iota(jnp.int32, logits.shape, 1)
    m1 = jnp.max(logits, axis=-1, keepdims=True)
    i1 = jnp.min(jnp.where(logits == m1, lane, n_e), axis=-1, keepdims=True)
    rest = jnp.where(lane == i1, -jnp.inf, logits)
    m2 = jnp.max(rest, axis=-1, keepdims=True)
    i2 = jnp.min(jnp.where(rest == m2, lane, n_e), axis=-1, keepdims=True)
    e2 = jnp.exp(m2 - m1)
    denom = 1.0 + e2

    sel = jnp.where((lane == i1) | (lane == i2), 1.0, 0.0)
    before = (lax.broadcasted_iota(jnp.int32, (tm, tm), 0)
              > lax.broadcasted_iota(jnp.int32, (tm, tm), 1))
    cum = jnp.dot(jnp.where(before, 1.0, 0.0).astype(BF16), sel.astype(BF16),
                  preferred_element_type=F32) + run_scr[...]
    r1 = jnp.sum(jnp.where(lane == i1, cum, 0.0), axis=-1, keepdims=True)
    r2 = jnp.sum(jnp.where(lane == i2, cum, 0.0), axis=-1, keepdims=True)
    run_scr[...] += jnp.sum(sel, axis=0, keepdims=True)

    idx_ref[...] = jnp.concatenate([i1, i2], axis=1)
    rank_ref[...] = jnp.concatenate([r1, r2], axis=1).astype(jnp.int32)
    w_ref[...] = jnp.concatenate([1.0 / denom, e2 / denom], axis=1)
    cnt_ref[...] = run_scr[...].astype(jnp.int32)


def _route(x2d, g, router_w, router_b, *, tm=512):
    t, d = x2d.shape
    n_e = router_w.shape[1]
    pair = pl.BlockSpec((tm, TOP_K), lambda i: (i, 0))
    return pl.pallas_call(
        _route_kernel,
        out_shape=(jax.ShapeDtypeStruct((t, TOP_K), jnp.int32),
                   jax.ShapeDtypeStruct((t, TOP_K), jnp.int32),
                   jax.ShapeDtypeStruct((t, TOP_K), F32),
                   jax.ShapeDtypeStruct((1, n_e), jnp.int32)),
        grid=(t // tm,),
        in_specs=[pl.BlockSpec((tm, d), lambda i: (i, 0)), _resident((1, d)),
                  _resident(router_w.shape), _resident((1, n_e))],
        out_specs=(pair, pair, pair, pl.BlockSpec((1, n_e), lambda i: (0, 0))),
        scratch_shapes=[pltpu.VMEM((1, n_e), F32)],
        compiler_params=_params(("arbitrary",)),
        name="moe_route",
    )(x2d, g, router_w, router_b)


def _wait_rows(hbm_ref, vmem_slot, sem, n_rows):
    pltpu.make_async_copy(hbm_ref.at[pl.ds(0, n_rows)], vmem_slot, sem).wait()


def _dispatch_kernel(pos_ref, x_ref, g_ref, zeros_ref, hs_ref, h_scr, sem):
    del zeros_ref
    i = pl.program_id(0)
    n = pl.num_programs(0)
    tm = x_ref.shape[0]
    slot = lax.rem(i, 2)

    def wait_slot(s):
        for _ in range(TOP_K):
            _wait_rows(hs_ref, h_scr.at[s], sem.at[s], tm)

    @pl.when(i >= 2)
    def _():
        wait_slot(slot)

    h_scr[slot] = _rmsnorm(x_ref[...], g_ref[...])
    base = i * (TOP_K * tm)

    def issue(r, carry):
        for k in range(TOP_K):
            p = pos_ref[base + TOP_K * r + k]
            pltpu.make_async_copy(h_scr.at[slot, pl.ds(r, 1)], hs_ref.at[pl.ds(p, 1)],
                                  sem.at[slot]).start()
        return carry

    lax.fori_loop(0, tm, issue, 0, unroll=8)

    @pl.when(i == n - 1)
    def _():
        wait_slot(slot)

    @pl.when((i == n - 1) & (n >= 2))
    def _():
        wait_slot(1 - slot)


def _dispatch(pos, x2d, g, n_rows, *, tm=256):
    t, d = x2d.shape
    zeros = jnp.zeros((n_rows, d), F32)
    return pl.pallas_call(
        _dispatch_kernel,
        out_shape=jax.ShapeDtypeStruct((n_rows, d), F32),
        grid_spec=pltpu.PrefetchScalarGridSpec(
            num_scalar_prefetch=1,
            grid=(t // tm,),
            in_specs=[pl.BlockSpec((tm, d), lambda i, pos: (i, 0)),
                      pl.BlockSpec((1, d), lambda i, pos: (0, 0)),
                      pl.BlockSpec(memory_space=pl.ANY)],
            out_specs=pl.BlockSpec(memory_space=pl.ANY),
            scratch_shapes=[pltpu.VMEM((2, tm, d), F32), pltpu.SemaphoreType.DMA((2,))],
        ),
        input_output_aliases={3: 0},
        compiler_params=_params(("arbitrary",)),
        name="moe_dispatch",
    )(pos, x2d, g, zeros)


def _experts_kernel(te_ref, nused_ref, hs_ref, wg_ref, wu_ref, wd_ref, y_ref, *, f_chunk):
    del te_ref
    i = pl.program_id(0)

    @pl.when(i < nused_ref[0])
    def _():
        h = hs_ref[...].astype(BF16)
        for c in range(0, wg_ref.shape[1], f_chunk):
            a = jnp.dot(h, wg_ref[:, c:c + f_chunk], preferred_element_type=F32)
            b = jnp.dot(h, wu_ref[:, c:c + f_chunk], preferred_element_type=F32)
            act = (jax.nn.silu(a) * b).astype(BF16)
            y = jnp.dot(act, wd_ref[c:c + f_chunk, :], preferred_element_type=F32)
            if c == 0:
                y_ref[...] = y
            else:
                y_ref[...] += y

    @pl.when(i >= nused_ref[0])
    def _():
        y_ref[...] = jnp.zeros_like(y_ref)


def _experts(tile_expert, n_used, hs, w_gate, w_up, w_down, *, tm, f_chunk=512):
    n_rows, d = hs.shape
    f_dim = w_gate.shape[2]

    def row_blk(i, te, nu):
        return (jnp.maximum(jnp.minimum(i, nu[0] - 1), 0), 0)

    expert_blk = lambda i, te, nu: (te[i], 0, 0)
    return pl.pallas_call(
        functools.partial(_experts_kernel, f_chunk=f_chunk),
        out_shape=jax.ShapeDtypeStruct((n_rows, d), F32),
        grid_spec=pltpu.PrefetchScalarGridSpec(
            num_scalar_prefetch=2,
            grid=(n_rows // tm,),
            in_specs=[
                pl.BlockSpec((tm, d), row_blk),
                pl.BlockSpec((None, d, f_dim), expert_blk),
                pl.BlockSpec((None, d, f_dim), expert_blk),
                pl.BlockSpec((None, f_dim, d), expert_blk),
            ],
            out_specs=pl.BlockSpec((tm, d), lambda i, te, nu: (i, 0)),
        ),
        compiler_params=pltpu.CompilerParams(dimension_semantics=("arbitrary",),
                                             vmem_limit_bytes=EXPERTS_VMEM_LIMIT_BYTES),
        name="moe_experts",
    )(tile_expert, n_used, hs, w_gate, w_up, w_down)


def _combine_kernel(pos_ref, x_ref, w_ref, fg_ref, y_ref, o_ref, buf, sem):
    i = pl.program_id(0)
    n = pl.num_programs(0)
    tm = x_ref.shape[0]
    slot = lax.rem(i, 2)

    def gather(tile, s):
        base = tile * (TOP_K * tm)

        def issue(r, carry):
            for k in range(TOP_K):
                p = pos_ref[base + TOP_K * r + k]
                pltpu.make_async_copy(y_ref.at[pl.ds(p, 1)], buf.at[s, k, pl.ds(r, 1)],
                                      sem.at[s]).start()
            return carry

        lax.fori_loop(0, tm, issue, 0, unroll=8)

    @pl.when(i == 0)
    def _():
        gather(0, 0)

    @pl.when(i + 1 < n)
    def _():
        gather(i + 1, 1 - slot)

    for k in range(TOP_K):
        _wait_rows(y_ref, buf.at[slot, k], sem.at[slot], tm)

    w = w_ref[...]
    y = x_ref[...] + (w[:, 0:1] * buf[slot, 0] + w[:, 1:2] * buf[slot, 1])
    o_ref[...] = _rmsnorm(y, fg_ref[...])


def _combine(pos, x2d, w, final_g, y_sorted, *, tm=256):
    t, d = x2d.shape
    return pl.pallas_call(
        _combine_kernel,
        out_shape=jax.ShapeDtypeStruct((t, d), F32),
        grid_spec=pltpu.PrefetchScalarGridSpec(
            num_scalar_prefetch=1,
            grid=(t // tm,),
            in_specs=[pl.BlockSpec((tm, d), lambda i, pos: (i, 0)),
                      pl.BlockSpec((tm, TOP_K), lambda i, pos: (i, 0)),
                      pl.BlockSpec((1, d), lambda i, pos: (0, 0)),
                      pl.BlockSpec(memory_space=pl.ANY)],
            out_specs=pl.BlockSpec((tm, d), lambda i, pos: (i, 0)),
            scratch_shapes=[pltpu.VMEM((2, TOP_K, tm, d), F32), pltpu.SemaphoreType.DMA((2,))],
        ),
        compiler_params=_params(("arbitrary",)),
        name="moe_combine",
    )(pos, x2d, w, final_g, y_sorted)


def _moe_layer(x2d, g, router_w, router_b, w_gate, w_up, w_down, final_g, *, row_tile=512):
    t, d = x2d.shape
    n_e = router_w.shape[1]
    idx, rank, w, counts = _route(x2d, g, router_w, router_b)

    counts = counts[0]
    padded = (counts + row_tile - 1) // row_tile * row_tile
    ends = jnp.cumsum(padded)
    starts = ends - padded
    experts = jnp.arange(n_e, dtype=jnp.int32)
    pos = rank + jnp.sum(jnp.where(idx[..., None] == experts, starts, 0), axis=-1)
    pos = pos.reshape(-1).astype(jnp.int32)

    n_tiles = (TOP_K * t) // row_tile + n_e
    tile_ends = ends // row_tile
    n_used = tile_ends[-1:].astype(jnp.int32)
    tile_ids = jnp.arange(n_tiles, dtype=jnp.int32)
    tile_expert = jnp.sum(jnp.minimum(tile_ids, n_used - 1)[:, None] >= tile_ends[None, :],
                          axis=1).astype(jnp.int32)

    hs = _dispatch(pos, x2d, g, n_tiles * row_tile)
    y_sorted = _experts(tile_expert, n_used, hs, w_gate, w_up, w_down, tm=row_tile)
    return _combine(pos, x2d, w, final_g, y_sorted)


def kernel(x, norm_mix_g, norm_ffn_g, final_g, a_w_in, a_v_gain, a_w_s, a_b_s, a_w_out,
           a_ffn_gate, a_ffn_up, a_ffn_down, b_w_qkv, b_rel_bias, b_w_o, b_router_w,
           b_router_b, b_exp_gate, b_exp_up, b_exp_down):
    batch, seq, d = x.shape
    assert norm_mix_g.shape[0] == 2, "two layers: gMLP/SwiGLU then attention/MoE"
    head_dim = d // ATT_HEADS
    tq = 256
    xf = x.reshape(batch * seq, d)
    row = lambda vec: vec.reshape(1, -1).astype(F32)

    xf = _gmlp_layer(xf, row(norm_mix_g[0]), a_w_in[0].astype(BF16), row(a_v_gain[0]),
                     a_w_s[0], a_b_s[0].T, a_w_out[0].astype(BF16))
    xf = _swiglu_layer(xf, row(norm_ffn_g[0]), a_ffn_gate[0].astype(BF16),
                       a_ffn_up[0].astype(BF16), a_ffn_down[0].astype(BF16))

    w_qkv = b_w_qkv[0].astype(BF16)
    q_t, k, v_t = _qkv_layer(xf, row(norm_mix_g[1]), w_qkv[:, :d].T, w_qkv[:, d:2 * d],
                             w_qkv[:, 2 * d:].T, q_scale=head_dim ** -0.5)
    bias_t = _band_bias(b_rel_bias[0], tq)
    xf = _attn_layer(xf, q_t, k, v_t, bias_t, b_w_o[0].astype(BF16).T, batch=batch, tq=tq)
    xf = _moe_layer(xf, row(norm_ffn_g[1]), b_router_w[0], row(b_router_b[0]),
                    b_exp_gate[0].astype(BF16), b_exp_up[0].astype(BF16),
                    b_exp_down[0].astype(BF16), row(final_g))
    return xf.reshape(batch, seq, d)
```

```python
import functools

import jax
import jax.numpy as jnp
from jax import lax
from jax.experimental import pallas as pl
from jax.experimental.pallas import tpu as pltpu

EPS = 1e-6
CHUNK = 64
GMLP_BLOCK = 128
GMLP_GROUPS = 8
ATT_HEADS = 16
ATT_LEFT_CHUNKS = 8
REL_CLIP = 256
TOP_K = 2
NEG_INF = -1e30
SUBLANES = 8

VMEM_LIMIT_BYTES = 56 * 1024 * 1024
EXPERTS_VMEM_LIMIT_BYTES = 60 * 1024 * 1024

F32 = jnp.float32
BF16 = jnp.bfloat16


def _params(semantics):
    return pltpu.CompilerParams(dimension_semantics=semantics,
                                vmem_limit_bytes=VMEM_LIMIT_BYTES)


def _resident(shape):
    zeros = (0,) * len(shape)
    return pl.BlockSpec(shape, lambda *_: zeros, pipeline_mode=pl.Buffered(1))


def _rmsnorm(x, g):
    ms = jnp.mean(x * x, axis=-1, keepdims=True)
    return x * lax.rsqrt(ms + EPS) * g


def _gelu_exact(z):
    return 0.5 * z * (1.0 + lax.erf(z * (2.0 ** -0.5)))


def _gmlp_kernel(x_ref, g_ref, win_ref, vgain_ref, ws_ref, bst_ref, wout_ref,
                 o_ref, u_scr, v_scr, gated_scr, *, n_chunk):
    tm = x_ref.shape[0]
    half = u_scr.shape[1]
    group_dim = half // GMLP_GROUPS
    x = x_ref[...]
    h = _rmsnorm(x, g_ref[...]).astype(BF16)

    ssq = jnp.zeros((tm, 1), F32)
    for c in range(0, 2 * half, n_chunk):
        z = jnp.dot(h, win_ref[:, c:c + n_chunk], preferred_element_type=F32)
        z = _gelu_exact(z)
        if c < half:
            u_scr[:, c:c + n_chunk] = z
        else:
            v_scr[:, c - half:c - half + n_chunk] = z
            ssq = ssq + jnp.sum(z * z, axis=-1, keepdims=True)
    inv = lax.rsqrt(ssq * (1.0 / half) + EPS)

    row_chunk = lax.broadcasted_iota(jnp.int32, (GMLP_BLOCK, GMLP_BLOCK), 0) // CHUNK
    col_chunk = lax.broadcasted_iota(jnp.int32, (GMLP_BLOCK, GMLP_BLOCK), 1) // CHUNK
    causal = row_chunk >= col_chunk
    for g in range(GMLP_GROUPS):
        cols = slice(g * group_dim, (g + 1) * group_dim)
        w_g = jnp.where(causal, ws_ref[g], 0.0).astype(BF16)
        b_g = bst_ref[:, g:g + 1]
        for n in range(tm // GMLP_BLOCK):
            rows = slice(n * GMLP_BLOCK, (n + 1) * GMLP_BLOCK)
            vn = (v_scr[rows, cols] * inv[rows] * vgain_ref[:, cols]).astype(BF16)
            sv = jnp.dot(w_g, vn, preferred_element_type=F32) + b_g
            gated_scr[rows, cols] = (u_scr[rows, cols] * sv).astype(BF16)

    out = jnp.dot(gated_scr[...], wout_ref[...], preferred_element_type=F32)
    o_ref[...] = x + out


def _gmlp_layer(x2d, g, w_in, v_gain, w_s, b_s_t, w_out, *, tm=512, n_chunk=512):
    t, d = x2d.shape
    half = w_out.shape[0]
    return pl.pallas_call(
        functools.partial(_gmlp_kernel, n_chunk=n_chunk),
        out_shape=jax.ShapeDtypeStruct((t, d), F32),
        grid=(t // tm,),
        in_specs=[
            pl.BlockSpec((tm, d), lambda i: (i, 0)),
            _resident((1, d)),
            _resident(w_in.shape),
            _resident((1, half)),
            _resident(w_s.shape),
            _resident(b_s_t.shape),
            _resident(w_out.shape),
        ],
        out_specs=pl.BlockSpec((tm, d), lambda i: (i, 0)),
        scratch_shapes=[
            pltpu.VMEM((tm, half), F32),
            pltpu.VMEM((tm, half), F32),
            pltpu.VMEM((tm, half), BF16),
        ],
        compiler_params=_params(("parallel",)),
        name="gmlp_mixer",
    )(x2d, g, w_in, v_gain, w_s, b_s_t, w_out)


def _swiglu_kernel(x_ref, g_ref, wg_ref, wu_ref, wd_ref, o_ref, *, f_chunk):
    x = x_ref[...]
    h = _rmsnorm(x, g_ref[...]).astype(BF16)
    o_ref[...] = x
    for c in range(0, wg_ref.shape[1], f_chunk):
        a = jnp.dot(h, wg_ref[:, c:c + f_chunk], preferred_element_type=F32)
        b = jnp.dot(h, wu_ref[:, c:c + f_chunk], preferred_element_type=F32)
        act = (jax.nn.silu(a) * b).astype(BF16)
        o_ref[...] += jnp.dot(act, wd_ref[c:c + f_chunk, :], preferred_element_type=F32)


def _swiglu_layer(x2d, g, w_gate, w_up, w_down, *, tm=512, f_chunk=256):
    t, d = x2d.shape
    return pl.pallas_call(
        functools.partial(_swiglu_kernel, f_chunk=f_chunk),
        out_shape=jax.ShapeDtypeStruct((t, d), F32),
        grid=(t // tm,),
        in_specs=[
            pl.BlockSpec((tm, d), lambda i: (i, 0)),
            _resident((1, d)),
            _resident(w_gate.shape),
            _resident(w_up.shape),
            _resident(w_down.shape),
        ],
        out_specs=pl.BlockSpec((tm, d), lambda i: (i, 0)),
        compiler_params=_params(("parallel",)),
        name="dense_swiglu",
    )(x2d, g, w_gate, w_up, w_down)


def _qkv_kernel(x_ref, g_ref, wqt_ref, wk_ref, wvt_ref, qt_ref, k_ref, vt_ref, *, q_scale):
    h = _rmsnorm(x_ref[...], g_ref[...]).astype(BF16)
    nt_dims = (((1,), (1,)), ((), ()))
    qt = lax.dot_general(wqt_ref[...], h, nt_dims, preferred_element_type=F32)
    qt_ref[...] = (qt * q_scale).astype(BF16)
    k_ref[...] = jnp.dot(h, wk_ref[...], preferred_element_type=F32).astype(BF16)
    vt_ref[...] = lax.dot_general(wvt_ref[...], h, nt_dims,
                                  preferred_element_type=F32).astype(BF16)


def _qkv_layer(x2d, g, w_q_t, w_k, w_v_t, *, q_scale, tm=512):
    t, d = x2d.shape
    row_spec = pl.BlockSpec((tm, d), lambda i: (i, 0))
    col_spec = pl.BlockSpec((d, tm), lambda i: (0, i))
    return pl.pallas_call(
        functools.partial(_qkv_kernel, q_scale=q_scale),
        out_shape=(jax.ShapeDtypeStruct((d, t), BF16), jax.ShapeDtypeStruct((t, d), BF16),
                   jax.ShapeDtypeStruct((d, t), BF16)),
        grid=(t // tm,),
        in_specs=[row_spec, _resident((1, d)), _resident((d, d)), _resident((d, d)),
                  _resident((d, d))],
        out_specs=(col_spec, row_spec, col_spec),
        compiler_params=_params(("parallel",)),
        name="qkv_proj",
    )(x2d, g, w_q_t, w_k, w_v_t)


def _attn_kernel(x_ref, qt_ref, k0_ref, k1_ref, k2_ref, vt0_ref, vt1_ref, vt2_ref,
                 bias_ref, wot_ref, o_ref, ctx_scr, *, head_dim):
    tq = qt_ref.shape[1]
    i = pl.program_id(1)
    k_refs = (k0_ref, k1_ref, k2_ref)
    vt_refs = (vt0_ref, vt1_ref, vt2_ref)
    n_win = len(k_refs) * tq
    half_q = tq // 2
    assert (ATT_LEFT_CHUNKS + 1) * CHUNK + half_q - CHUNK <= n_win - half_q
    key_pos = lax.broadcasted_iota(jnp.int32, (n_win, 1), 0) + (i - 2) * tq
    in_seq = key_pos >= 0
    zero_half = jnp.zeros((head_dim, tq), BF16)

    def scores(h):
        pair = slice((h // 2) * 2 * head_dim, (h // 2 + 1) * 2 * head_dim)
        qt_h = qt_ref[h * head_dim:(h + 1) * head_dim, :]
        rhs = jnp.concatenate([qt_h, zero_half] if h % 2 == 0 else [zero_half, qt_h], axis=0)
        return jnp.concatenate(
            [jnp.dot(kr[:, pair], rhs, preferred_element_type=F32) for kr in k_refs],
            axis=0)

    def all_heads(window_has_padding):
        s_next = scores(0)
        for h in range(ATT_HEADS):
            rows = slice(h * head_dim, (h + 1) * head_dim)
            s = s_next
            if h + 1 < ATT_HEADS:
                s_next = scores(h + 1)
            halves = []
            for lanes, keys in ((slice(0, half_q), slice(0, n_win - half_q)),
                                (slice(half_q, tq), slice(half_q, n_win))):
                sh = s[keys, lanes] + bias_ref[h, keys, lanes]
                if window_has_padding:
                    sh = jnp.where(in_seq[keys], sh, NEG_INF)
                m = jnp.max(sh, axis=0, keepdims=True)
                p = jnp.exp(sh - m)
                halves.append((jnp.sum(p, axis=0, keepdims=True), p.astype(BF16)))
            unseen = jnp.zeros((half_q, half_q), BF16)
            l = jnp.concatenate([halves[0][0], halves[1][0]], axis=1)
            pb = jnp.concatenate([jnp.concatenate([halves[0][1], unseen], axis=0),
                                  jnp.concatenate([unseen, halves[1][1]], axis=0)], axis=1)
            ctx = jnp.zeros((head_dim, tq), F32)
            for j, vr in enumerate(vt_refs):
                ctx = ctx + jnp.dot(vr[rows, :], pb[j * tq:(j + 1) * tq, :],
                                    preferred_element_type=F32)
            ctx_scr[rows, :] = (ctx / l).astype(BF16)
        out_t = jnp.dot(wot_ref[...], ctx_scr[...], preferred_element_type=F32)
        o_ref[...] = x_ref[...] + out_t.T

    pl.when(i < 2)(functools.partial(all_heads, True))
    pl.when(i >= 2)(functools.partial(all_heads, False))


def _attn_layer(x2d, q_t, k, v_t, bias_t, w_o_t, *, batch, tq=256):
    t, d = x2d.shape
    nt = t // batch // tq

    def cur(b, i):
        return b * nt + i

    def back(n):
        return lambda b, i: b * nt + jnp.maximum(i - n, 0)

    rows = lambda f: pl.BlockSpec((tq, d), lambda b, i: (f(b, i), 0))
    cols = lambda f: pl.BlockSpec((d, tq), lambda b, i: (0, f(b, i)))
    return pl.pallas_call(
        functools.partial(_attn_kernel, head_dim=d // ATT_HEADS),
        out_shape=jax.ShapeDtypeStruct((t, d), F32),
        grid=(batch, nt),
        in_specs=[rows(cur), cols(cur),
                  rows(back(2)), rows(back(1)), rows(cur),
                  cols(back(2)), cols(back(1)), cols(cur),
                  _resident(bias_t.shape), _resident(w_o_t.shape)],
        out_specs=rows(cur),
        scratch_shapes=[pltpu.VMEM((d, tq), BF16)],
        compiler_params=_params(("parallel", "arbitrary")),
        name="band_attention",
    )(x2d, q_t, k, k, k, v_t, v_t, v_t, bias_t, w_o_t)


def _band_bias(rel_table, tq):
    band = (ATT_LEFT_CHUNKS + 1) * CHUNK
    n_heads = rel_table.shape[0]
    assert 3 * tq >= band + tq - CHUNK and rel_table.shape[1] == REL_CLIP + CHUNK
    tail = jnp.broadcast_to(rel_table[:, -1:], (n_heads, band - 1 - REL_CLIP))
    rev = jnp.concatenate([rel_table, tail], axis=1).astype(F32)[:, ::-1]
    chunk = jnp.stack([rev[:, CHUNK - 1 - qi:CHUNK - 1 - qi + band] for qi in range(CHUNK)],
                      axis=1)
    lane_pad = -band % 128
    chunk = jnp.pad(chunk, ((0, 0), (0, 128 - CHUNK), (0, lane_pad)))
    return pl.pallas_call(
        functools.partial(_bias_tile_kernel, tq=tq, band=band),
        out_shape=jax.ShapeDtypeStruct((n_heads, 3 * tq, tq), F32),
        grid=(n_heads,),
        in_specs=[pl.BlockSpec((None, 128, band + lane_pad), lambda h: (h, 0, 0))],
        out_specs=pl.BlockSpec((None, 3 * tq, tq), lambda h: (h, 0, 0)),
        compiler_params=_params(("parallel",)),
        name="band_bias_tile",
    )(chunk)


def _bias_tile_kernel(chunk_ref, o_ref, *, tq, band):
    chunk_t = chunk_ref[...].T[:band, :CHUNK]
    cols = []
    for c in range(tq // CHUNK):
        parts = [chunk_t]
        if c > 0:
            parts.insert(0, jnp.full((c * CHUNK, CHUNK), NEG_INF, F32))
        below = 3 * tq - band - c * CHUNK
        if below > 0:
            parts.append(jnp.full((below, CHUNK), NEG_INF, F32))
        cols.append(jnp.concatenate(parts, axis=0))
    o_ref[...] = jnp.concatenate(cols, axis=1)


def _route_kernel(x_ref, g_ref, rw_ref, rb_ref, idx_ref, rank_ref, w_ref, cnt_ref, off_ref,
                  run_scr):
    tm = x_ref.shape[0]

    @pl.when(pl.program_id(0) == 0)
    def _():
        run_scr[...] = jnp.zeros_like(run_scr)

    off_ref[...] = run_scr[...].astype(jnp.int32)

    hf = _rmsnorm(x_ref[...], g_ref[...])
    logits = jnp.dot(hf, rw_ref[...], preferred_element_type=F32,
                     precision=lax.Precision.HIGHEST) + rb_ref[...]
    n_e = logits.shape[-1]
    lane = lax.broadcasted_iota(jnp.int32, logits.shape, 1)
    m1 = jnp.max(logits, axis=-1, keepdims=True)
    i1 = jnp.min(jnp.where(logits == m1, lane, n_e), axis=-1, keepdims=True)
    rest = jnp.where(lane == i1, -jnp.inf, logits)
    m2 = jnp.max(rest, axis=-1, keepdims=True)
    i2 = jnp.min(jnp.where(rest == m2, lane, n_e), axis=-1, keepdims=True)
    e2 = jnp.exp(m2 - m1)
    denom = 1.0 + e2

    sel = jnp.where((lane == i1) | (lane == i2), 1.0, 0.0)
    before = (lax.broadcasted_iota(jnp.int32, (tm, tm), 0)
              > lax.broadcasted_iota(jnp.int32, (tm, tm), 1))
    cum = jnp.dot(jnp.where(before, 1.0, 0.0).astype(BF16), sel.astype(BF16),
                  preferred_element_type=F32) + run_scr[...]
    r1 = jnp.sum(jnp.where(lane == i1, cum, 0.0), axis=-1, keepdims=True)
    r2 = jnp.sum(jnp.where(lane == i2, cum, 0.0), axis=-1, keepdims=True)
    run_scr[...] += jnp.sum(sel, axis=0, keepdims=True)

    idx_ref[...] = jnp.concatenate([i1, i2], axis=1)
    rank_ref[...] = jnp.concatenate([r1, r2], axis=1).astype(jnp.int32)
    w_ref[...] = jnp.concatenate([1.0 / denom, e2 / denom], axis=1)
    cnt_ref[...] = run_scr[...].astype(jnp.int32)


def _route(x2d, g, router_w, router_b, *, tm):
    t, d = x2d.shape
    n_e = router_w.shape[1]
    pair = pl.BlockSpec((tm, TOP_K), lambda i: (i, 0))
    return pl.pallas_call(
        _route_kernel,
        out_shape=(jax.ShapeDtypeStruct((t, TOP_K), jnp.int32),
                   jax.ShapeDtypeStruct((t, TOP_K), jnp.int32),
                   jax.ShapeDtypeStruct((t, TOP_K), F32),
                   jax.ShapeDtypeStruct((1, n_e), jnp.int32),
                   jax.ShapeDtypeStruct((t // tm, 1, n_e), jnp.int32)),
        grid=(t // tm,),
        in_specs=[pl.BlockSpec((tm, d), lambda i: (i, 0)), _resident((1, d)),
                  _resident(router_w.shape), _resident((1, n_e))],
        out_specs=(pair, pair, pair, pl.BlockSpec((1, n_e), lambda i: (0, 0)),
                   pl.BlockSpec((None, 1, n_e), lambda i: (i, 0, 0))),
        scratch_shapes=[pltpu.VMEM((1, n_e), F32)],
        compiler_params=_params(("arbitrary",)),
        name="moe_route",
    )(x2d, g, router_w, router_b)


def _wait_rows(hbm_ref, vmem_slot, sem, n_rows):
    pltpu.make_async_copy(hbm_ref.at[pl.ds(0, n_rows)], vmem_slot, sem).wait()


def _dispatch_kernel(lstart_ref, cnt_ref, gdst_ref, padst_ref, nused_ref, x_ref, g_ref, lpos_ref,
                     hs_ref, buf, zero_scr, sem, zero_sem, *, n_e, row_tile, min_used_tiles):
    i = pl.program_id(0)
    n = pl.num_programs(0)
    tb = x_ref.shape[0]
    n_loc = buf.shape[1]
    slot = lax.rem(i, 2)

    def run_copies(blk, s, act):
        for e in range(n_e):
            lstart = lstart_ref[blk * n_e + e]
            cnt = cnt_ref[blk * n_e + e]
            gdst = gdst_ref[blk * n_e + e]
            for log_len in range((tb + SUBLANES).bit_length() - 1, SUBLANES.bit_length() - 2, -1):
                length = 1 << log_len
                done = (cnt >> (log_len + 1)) << (log_len + 1)

                @pl.when((cnt & length) != 0)
                def _(lstart=lstart, gdst=gdst, done=done, length=length):
                    src = buf.at[s, pl.ds(pl.multiple_of(lstart + done, SUBLANES), length)]
                    dst = hs_ref.at[pl.ds(pl.multiple_of(gdst + done, SUBLANES), length)]
                    act(pltpu.make_async_copy(src, dst, sem.at[s]))

    @pl.when(i == 0)
    def _():
        zero_scr[...] = jnp.zeros_like(zero_scr)
        pads = [pltpu.make_async_copy(
            zero_scr, hs_ref.at[pl.ds(pl.multiple_of(padst_ref[e], SUBLANES), row_tile)], zero_sem)
            for e in range(n_e)]
        for cp in pads:
            cp.start()
        for cp in pads:
            cp.wait()
        for act in (lambda cp: cp.start(), lambda cp: cp.wait()):
            for j in range(min_used_tiles, hs_ref.shape[0] // row_tile):
                @pl.when(j >= nused_ref[0])
                def _(j=j, act=act):
                    act(pltpu.make_async_copy(
                        zero_scr, hs_ref.at[pl.ds(j * row_tile, row_tile)], zero_sem))

    @pl.when(i >= 2)
    def _():
        run_copies(i - 2, slot, lambda cp: cp.wait())

    h = _rmsnorm(x_ref[...], g_ref[...]).astype(BF16)
    row = lax.broadcasted_iota(jnp.int32, (n_loc, tb), 0)
    lpos = lpos_ref[...]
    picks = (row == lpos[0:1, :]) | (row == lpos[1:2, :])
    buf[slot] = jnp.dot(jnp.where(picks, 1.0, 0.0).astype(BF16), h, preferred_element_type=F32)
    run_copies(i, slot, lambda cp: cp.start())

    @pl.when(i == n - 1)
    def _():
        run_copies(i, slot, lambda cp: cp.wait())

    @pl.when((i == n - 1) & (n >= 2))
    def _():
        run_copies(i - 1, 1 - slot, lambda cp: cp.wait())


def _dispatch(lstart, cnt, gdst, pad_start, n_used, x2d, g, lpos_t, n_rows, *, tb, n_e, row_tile):
    t, d = x2d.shape
    return pl.pallas_call(
        functools.partial(_dispatch_kernel, n_e=n_e, row_tile=row_tile,
                          min_used_tiles=(TOP_K * t) // row_tile),
        out_shape=jax.ShapeDtypeStruct((n_rows + row_tile, d), F32),
        grid_spec=pltpu.PrefetchScalarGridSpec(
            num_scalar_prefetch=5,
            grid=(t // tb,),
            in_specs=[pl.BlockSpec((tb, d), lambda i, *_: (i, 0)),
                      pl.BlockSpec((1, d), lambda i, *_: (0, 0)),
                      pl.BlockSpec((TOP_K, tb), lambda i, *_: (0, i))],
            out_specs=pl.BlockSpec(memory_space=pl.ANY),
            scratch_shapes=[pltpu.VMEM((2, TOP_K * tb + SUBLANES * n_e, d), F32),
                            pltpu.VMEM((row_tile, d), F32),
                            pltpu.SemaphoreType.DMA((2,)), pltpu.SemaphoreType.DMA(())],
        ),
        compiler_params=_params(("arbitrary",)),
        name="moe_dispatch",
    )(lstart, cnt, gdst, pad_start, n_used, x2d, g, lpos_t)


def _experts_kernel(te_ref, nused_ref, hs_ref, wg_ref, wu_ref, wd_ref, y_ref, *, f_chunk):
    del te_ref
    i = pl.program_id(0)

    @pl.when(i < nused_ref[0])
    def _():
        h = hs_ref[...].astype(BF16)
        for c in range(0, wg_ref.shape[1], f_chunk):
            a = jnp.dot(h, wg_ref[:, c:c + f_chunk], preferred_element_type=F32)
            b = jnp.dot(h, wu_ref[:, c:c + f_chunk], preferred_element_type=F32)
            act = (jax.nn.silu(a) * b).astype(BF16)
            y = jnp.dot(act, wd_ref[c:c + f_chunk, :], preferred_element_type=F32)
            if c == 0:
                y_ref[...] = y
            else:
                y_ref[...] += y

    @pl.when(i >= nused_ref[0])
    def _():
        y_ref[...] = jnp.zeros_like(y_ref)


def _experts(tile_expert, n_used, hs, w_gate, w_up, w_down, *, tm, f_chunk=512):
    d = hs.shape[1]
    n_rows = tile_expert.shape[0] * tm
    f_dim = w_gate.shape[2]

    def row_blk(i, te, nu):
        return (jnp.maximum(jnp.minimum(i, nu[0] - 1), 0), 0)

    expert_blk = lambda i, te, nu: (te[i], 0, 0)
    return pl.pallas_call(
        functools.partial(_experts_kernel, f_chunk=f_chunk),
        out_shape=jax.ShapeDtypeStruct((n_rows, d), F32),
        grid_spec=pltpu.PrefetchScalarGridSpec(
            num_scalar_prefetch=2,
            grid=(n_rows // tm,),
            in_specs=[
                pl.BlockSpec((tm, d), row_blk),
                pl.BlockSpec((None, d, f_dim), expert_blk),
                pl.BlockSpec((None, d, f_dim), expert_blk),
                pl.BlockSpec((None, f_dim, d), expert_blk),
            ],
            out_specs=pl.BlockSpec((tm, d), lambda i, te, nu: (i, 0)),
        ),
        compiler_params=pltpu.CompilerParams(dimension_semantics=("arbitrary",),
                                             vmem_limit_bytes=EXPERTS_VMEM_LIMIT_BYTES),
        name="moe_experts",
    )(tile_expert, n_used, hs, w_gate, w_up, w_down)


def _combine_kernel(pos_ref, x_ref, w_ref, fg_ref, y_ref, o_ref, buf, sem):
    i = pl.program_id(0)
    n = pl.num_programs(0)
    tm = x_ref.shape[0]
    slot = lax.rem(i, 2)

    def gather(tile, s):
        base = tile * (TOP_K * tm)

        def issue(r, carry):
            for k in range(TOP_K):
                p = pos_ref[base + TOP_K * r + k]
                pltpu.make_async_copy(y_ref.at[pl.ds(p, 1)], buf.at[s, k, pl.ds(r, 1)],
                                      sem.at[s]).start()
            return carry

        lax.fori_loop(0, tm, issue, 0, unroll=8)

    @pl.when(i == 0)
    def _():
        gather(0, 0)

    @pl.when(i + 1 < n)
    def _():
        gather(i + 1, 1 - slot)

    for k in range(TOP_K):
        _wait_rows(y_ref, buf.at[slot, k], sem.at[slot], tm)

    w = w_ref[...]
    y = x_ref[...] + (w[:, 0:1] * buf[slot, 0] + w[:, 1:2] * buf[slot, 1])
    o_ref[...] = _rmsnorm(y, fg_ref[...])


def _combine(pos, x2d, w, final_g, y_sorted, *, tm=256):
    t, d = x2d.shape
    return pl.pallas_call(
        _combine_kernel,
        out_shape=jax.ShapeDtypeStruct((t, d), F32),
        grid_spec=pltpu.PrefetchScalarGridSpec(
            num_scalar_prefetch=1,
            grid=(t // tm,),
            in_specs=[pl.BlockSpec((tm, d), lambda i, pos: (i, 0)),
                      pl.BlockSpec((tm, TOP_K), lambda i, pos: (i, 0)),
                      pl.BlockSpec((1, d), lambda i, pos: (0, 0)),
                      pl.BlockSpec(memory_space=pl.ANY)],
            out_specs=pl.BlockSpec((tm, d), lambda i, pos: (i, 0)),
            scratch_shapes=[pltpu.VMEM((2, TOP_K, tm, d), F32), pltpu.SemaphoreType.DMA((2,))],
        ),
        compiler_params=_params(("arbitrary",)),
        name="moe_combine",
    )(pos, x2d, w, final_g, y_sorted)


def _moe_layer(x2d, g, router_w, router_b, w_gate, w_up, w_down, final_g, *,
               row_tile=512, tok_block=512):
    t, d = x2d.shape
    n_e = router_w.shape[1]
    n_blk = t // tok_block
    idx, rank, w, counts, blk_off = _route(x2d, g, router_w, router_b, tm=tok_block)

    counts = counts[0]
    blk_off = blk_off[:, 0, :]
    blk_cnt = jnp.concatenate([blk_off[1:], counts[None]], axis=0) - blk_off
    run_len = (blk_cnt + SUBLANES - 1) // SUBLANES * SUBLANES
    run_lstart = jnp.cumsum(run_len, axis=1) - run_len
    group_len = jnp.sum(run_len, axis=0)
    padded = (group_len + row_tile - 1) // row_tile * row_tile
    ends = jnp.cumsum(padded)
    starts = ends - padded
    run_gdst = starts[None, :] + jnp.cumsum(run_len, axis=0) - run_len

    experts = jnp.arange(n_e, dtype=jnp.int32)
    chosen_b = (idx[..., None] == experts).reshape(n_blk, tok_block, TOP_K, n_e)
    per_tok = lambda v: jnp.sum(jnp.where(chosen_b, v[:, None, None, :], 0), axis=-1)
    in_run = rank.reshape(n_blk, tok_block, TOP_K) - per_tok(blk_off)
    lpos_t = (in_run + per_tok(run_lstart)).reshape(t, TOP_K).T.astype(jnp.int32)
    pos = (in_run + per_tok(run_gdst)).reshape(-1).astype(jnp.int32)

    n_tiles = pl.cdiv(TOP_K * t + (SUBLANES - 1) * n_blk * n_e, row_tile) + n_e
    tile_ends = ends // row_tile
    n_used = tile_ends[-1:].astype(jnp.int32)
    tile_ids = jnp.arange(n_tiles, dtype=jnp.int32)
    tile_expert = jnp.sum(jnp.minimum(tile_ids, n_used - 1)[:, None] >= tile_ends[None, :],
                          axis=1).astype(jnp.int32)

    flat = lambda v: v.reshape(-1).astype(jnp.int32)
    hs = _dispatch(flat(run_lstart), flat(run_len), flat(run_gdst), flat(starts + group_len),
                   n_used, x2d, g, lpos_t, n_tiles * row_tile, tb=tok_block, n_e=n_e, row_tile=row_tile)
    y_sorted = _experts(tile_expert, n_used, hs, w_gate, w_up, w_down, tm=row_tile)
    return _combine(pos, x2d, w, final_g, y_sorted)


def kernel(x, norm_mix_g, norm_ffn_g, final_g, a_w_in, a_v_gain, a_w_s, a_b_s, a_w_out,
           a_ffn_gate, a_ffn_up, a_ffn_down, b_w_qkv, b_rel_bias, b_w_o, b_router_w,
           b_router_b, b_exp_gate, b_exp_up, b_exp_down):
    batch, seq, d = x.shape
    assert norm_mix_g.shape[0] == 2, "two layers: gMLP/SwiGLU then attention/MoE"
    head_dim = d // ATT_HEADS
    tq = 256
    xf = x.reshape(batch * seq, d)
    row = lambda vec: vec.reshape(1, -1).astype(F32)

    xf = _gmlp_layer(xf, row(norm_mix_g[0]), a_w_in[0].astype(BF16), row(a_v_gain[0]),
                     a_w_s[0], a_b_s[0].T, a_w_out[0].astype(BF16))
    xf = _swiglu_layer(xf, row(norm_ffn_g[0]), a_ffn_gate[0].astype(BF16),
                       a_ffn_up[0].astype(BF16), a_ffn_down[0].astype(BF16))

    w_qkv = b_w_qkv[0].astype(BF16)
    q_t, k, v_t = _qkv_layer(xf, row(norm_mix_g[1]), w_qkv[:, :d].T, w_qkv[:, d:2 * d],
                             w_qkv[:, 2 * d:].T, q_scale=head_dim ** -0.5)
    bias_t = _band_bias(b_rel_bias[0], tq)
    xf = _attn_layer(xf, q_t, k, v_t, bias_t, b_w_o[0].astype(BF16).T, batch=batch, tq=tq)
    xf = _moe_layer(xf, row(norm_ffn_g[1]), b_router_w[0], row(b_router_b[0]),
                    b_exp_gate[0].astype(BF16), b_exp_up[0].astype(BF16),
                    b_exp_down[0].astype(BF16), row(final_g))
    return xf.reshape(batch, seq, d)
```

```python
import functools

import jax
import jax.numpy as jnp
from jax import lax
from jax.experimental import pallas as pl
from jax.experimental.pallas import tpu as pltpu

EPS = 1e-6
CHUNK = 64
GMLP_BLOCK = 128
GMLP_GROUPS = 8
ATT_HEADS = 16
ATT_LEFT_CHUNKS = 8
REL_CLIP = 256
TOP_K = 2
NEG_INF = -1e30
SUBLANES = 8

VMEM_LIMIT_BYTES = 56 * 1024 * 1024
EXPERTS_VMEM_LIMIT_BYTES = 60 * 1024 * 1024
WEIGHT_STAGE_BYTES = 2 * 1024 * 1024

F32 = jnp.float32
BF16 = jnp.bfloat16


def _params(semantics):
    return pltpu.CompilerParams(dimension_semantics=semantics,
                                vmem_limit_bytes=VMEM_LIMIT_BYTES)


def _resident(shape):
    zeros = (0,) * len(shape)
    return pl.BlockSpec(shape, lambda *_: zeros, pipeline_mode=pl.Buffered(1))


_HBM = pl.BlockSpec(memory_space=pl.ANY)


def _stage_rows(w):
    rows = w.shape[0]
    while rows * w.shape[1] * 4 > WEIGHT_STAGE_BYTES and rows % 2 == 0:
        rows //= 2
    return rows


def _stage_scratch(w):
    return [pltpu.VMEM(w.shape, BF16), pltpu.VMEM((2, _stage_rows(w), w.shape[1]), F32),
            pltpu.SemaphoreType.DMA((2,))]


def _load_weight_bf16(w_hbm, w_bf, stage, sem):
    rows = stage.shape[1]
    n_chunks = w_hbm.shape[0] // rows
    chunk = lambda c: pltpu.make_async_copy(w_hbm.at[pl.ds(c * rows, rows)], stage.at[c % 2],
                                            sem.at[c % 2])
    chunk(0).start()
    for c in range(n_chunks):
        if c + 1 < n_chunks:
            chunk(c + 1).start()
        chunk(c).wait()
        w_bf[c * rows:(c + 1) * rows, :] = stage[c % 2].astype(BF16)


def _rmsnorm(x, g):
    ms = jnp.mean(x * x, axis=-1, keepdims=True)
    return x * lax.rsqrt(ms + EPS) * g


def _gelu_exact(z):
    return 0.5 * z * (1.0 + lax.erf(z * (2.0 ** -0.5)))


def _gmlp_kernel(x_ref, g_ref, win_hbm, vgain_ref, ws_ref, bst_ref, wout_hbm,
                 o_ref, u_scr, v_scr, gated_scr, win_ref, win_stage, win_sem,
                 wout_ref, wout_stage, wout_sem, *, n_chunk):
    tm = x_ref.shape[0]
    half = u_scr.shape[1]
    group_dim = half // GMLP_GROUPS

    @pl.when(pl.program_id(0) == 0)
    def _():
        _load_weight_bf16(win_hbm, win_ref, win_stage, win_sem)
        _load_weight_bf16(wout_hbm, wout_ref, wout_stage, wout_sem)

    x = x_ref[...]
    h = _rmsnorm(x, g_ref[...]).astype(BF16)

    ssq = jnp.zeros((tm, 1), F32)
    for c in range(0, 2 * half, n_chunk):
        z = jnp.dot(h, win_ref[:, c:c + n_chunk], preferred_element_type=F32)
        z = _gelu_exact(z)
        if c < half:
            u_scr[:, c:c + n_chunk] = z
        else:
            v_scr[:, c - half:c - half + n_chunk] = z
            ssq = ssq + jnp.sum(z * z, axis=-1, keepdims=True)
    inv = lax.rsqrt(ssq * (1.0 / half) + EPS)

    row_chunk = lax.broadcasted_iota(jnp.int32, (GMLP_BLOCK, GMLP_BLOCK), 0) // CHUNK
    col_chunk = lax.broadcasted_iota(jnp.int32, (GMLP_BLOCK, GMLP_BLOCK), 1) // CHUNK
    causal = row_chunk >= col_chunk
    for g in range(GMLP_GROUPS):
        cols = slice(g * group_dim, (g + 1) * group_dim)
        w_g = jnp.where(causal, ws_ref[g], 0.0).astype(BF16)
        b_g = bst_ref[:, g:g + 1]
        for n in range(tm // GMLP_BLOCK):
            rows = slice(n * GMLP_BLOCK, (n + 1) * GMLP_BLOCK)
            vn = (v_scr[rows, cols] * inv[rows] * vgain_ref[:, cols]).astype(BF16)
            sv = jnp.dot(w_g, vn, preferred_element_type=F32) + b_g
            gated_scr[rows, cols] = (u_scr[rows, cols] * sv).astype(BF16)

    out = jnp.dot(gated_scr[...], wout_ref[...], preferred_element_type=F32)
    o_ref[...] = x + out


def _gmlp_layer(x2d, g, w_in, v_gain, w_s, b_s_t, w_out, *, tm=512, n_chunk=512):
    t, d = x2d.shape
    half = w_out.shape[0]
    return pl.pallas_call(
        functools.partial(_gmlp_kernel, n_chunk=n_chunk),
        out_shape=jax.ShapeDtypeStruct((t, d), F32),
        grid=(t // tm,),
        in_specs=[
            pl.BlockSpec((tm, d), lambda i: (i, 0)),
            _resident((1, d)),
            _HBM,
            _resident((1, half)),
            _resident(w_s.shape),
            _resident(b_s_t.shape),
            _HBM,
        ],
        out_specs=pl.BlockSpec((tm, d), lambda i: (i, 0)),
        scratch_shapes=[
            pltpu.VMEM((tm, half), F32),
            pltpu.VMEM((tm, half), F32),
            pltpu.VMEM((tm, half), BF16),
            *_stage_scratch(w_in), *_stage_scratch(w_out),
        ],
        compiler_params=_params(("arbitrary",)),
        name="gmlp_mixer",
    )(x2d, g, w_in, v_gain, w_s, b_s_t, w_out)


def _swiglu_kernel(x_ref, g_ref, wg_hbm, wu_hbm, wd_hbm, o_ref, wg_ref, wg_stage, wg_sem,
                   wu_ref, wu_stage, wu_sem, wd_ref, wd_stage, wd_sem, *, f_chunk):
    @pl.when(pl.program_id(0) == 0)
    def _():
        _load_weight_bf16(wg_hbm, wg_ref, wg_stage, wg_sem)
        _load_weight_bf16(wu_hbm, wu_ref, wu_stage, wu_sem)
        _load_weight_bf16(wd_hbm, wd_ref, wd_stage, wd_sem)

    x = x_ref[...]
    h = _rmsnorm(x, g_ref[...]).astype(BF16)
    o_ref[...] = x
    for c in range(0, wg_ref.shape[1], f_chunk):
        a = jnp.dot(h, wg_ref[:, c:c + f_chunk], preferred_element_type=F32)
        b = jnp.dot(h, wu_ref[:, c:c + f_chunk], preferred_element_type=F32)
        act = (jax.nn.silu(a) * b).astype(BF16)
        o_ref[...] += jnp.dot(act, wd_ref[c:c + f_chunk, :], preferred_element_type=F32)


def _swiglu_layer(x2d, g, w_gate, w_up, w_down, *, tm=512, f_chunk=256):
    t, d = x2d.shape
    return pl.pallas_call(
        functools.partial(_swiglu_kernel, f_chunk=f_chunk),
        out_shape=jax.ShapeDtypeStruct((t, d), F32),
        grid=(t // tm,),
        in_specs=[pl.BlockSpec((tm, d), lambda i: (i, 0)), _resident((1, d)), _HBM, _HBM, _HBM],
        out_specs=pl.BlockSpec((tm, d), lambda i: (i, 0)),
        scratch_shapes=[*_stage_scratch(w_gate), *_stage_scratch(w_up), *_stage_scratch(w_down)],
        compiler_params=_params(("arbitrary",)),
        name="dense_swiglu",
    )(x2d, g, w_gate, w_up, w_down)


def _qkv_kernel(x_ref, g_ref, wqt_ref, wk_ref, wvt_ref, qt_ref, k_ref, vt_ref, *, q_scale):
    h = _rmsnorm(x_ref[...], g_ref[...]).astype(BF16)
    nt_dims = (((1,), (1,)), ((), ()))
    qt = lax.dot_general(wqt_ref[...], h, nt_dims, preferred_element_type=F32)
    qt_ref[...] = (qt * q_scale).astype(BF16)
    k_ref[...] = jnp.dot(h, wk_ref[...], preferred_element_type=F32).astype(BF16)
    vt_ref[...] = lax.dot_general(wvt_ref[...], h, nt_dims,
                                  preferred_element_type=F32).astype(BF16)


def _qkv_layer(x2d, g, w_q_t, w_k, w_v_t, *, q_scale, tm=512):
    t, d = x2d.shape
    row_spec = pl.BlockSpec((tm, d), lambda i: (i, 0))
    col_spec = pl.BlockSpec((d, tm), lambda i: (0, i))
    return pl.pallas_call(
        functools.partial(_qkv_kernel, q_scale=q_scale),
        out_shape=(jax.ShapeDtypeStruct((d, t), BF16), jax.ShapeDtypeStruct((t, d), BF16),
                   jax.ShapeDtypeStruct((d, t), BF16)),
        grid=(t // tm,),
        in_specs=[row_spec, _resident((1, d)), _resident((d, d)), _resident((d, d)),
                  _resident((d, d))],
        out_specs=(col_spec, row_spec, col_spec),
        compiler_params=_params(("parallel",)),
        name="qkv_proj",
    )(x2d, g, w_q_t, w_k, w_v_t)


def _attn_kernel(x_ref, qt_ref, k0_ref, k1_ref, k2_ref, vt0_ref, vt1_ref, vt2_ref,
                 bias_ref, wot_ref, o_ref, ctx_scr, *, head_dim):
    tq = qt_ref.shape[1]
    i = pl.program_id(1)
    k_refs = (k0_ref, k1_ref, k2_ref)
    vt_refs = (vt0_ref, vt1_ref, vt2_ref)
    n_win = len(k_refs) * tq
    half_q = tq // 2
    assert (ATT_LEFT_CHUNKS + 1) * CHUNK + half_q - CHUNK <= n_win - half_q
    key_pos = lax.broadcasted_iota(jnp.int32, (n_win, 1), 0) + (i - 2) * tq
    in_seq = key_pos >= 0
    zero_half = jnp.zeros((head_dim, tq), BF16)

    def scores(h):
        pair = slice((h // 2) * 2 * head_dim, (h // 2 + 1) * 2 * head_dim)
        qt_h = qt_ref[h * head_dim:(h + 1) * head_dim, :]
        rhs = jnp.concatenate([qt_h, zero_half] if h % 2 == 0 else [zero_half, qt_h], axis=0)
        return jnp.concatenate(
            [jnp.dot(kr[:, pair], rhs, preferred_element_type=F32) for kr in k_refs],
            axis=0)

    def all_heads(window_has_padding):
        s_next = scores(0)
        for h in range(ATT_HEADS):
            rows = slice(h * head_dim, (h + 1) * head_dim)
            s = s_next
            if h + 1 < ATT_HEADS:
                s_next = scores(h + 1)
            halves = []
            for lanes, keys in ((slice(0, half_q), slice(0, n_win - half_q)),
                                (slice(half_q, tq), slice(half_q, n_win))):
                sh = s[keys, lanes] + bias_ref[h, keys, lanes]
                if window_has_padding:
                    sh = jnp.where(in_seq[keys], sh, NEG_INF)
                m = jnp.max(sh, axis=0, keepdims=True)
                p = jnp.exp(sh - m)
                halves.append((jnp.sum(p, axis=0, keepdims=True), p.astype(BF16)))
            unseen = jnp.zeros((half_q, half_q), BF16)
            l = jnp.concatenate([halves[0][0], halves[1][0]], axis=1)
            pb = jnp.concatenate([jnp.concatenate([halves[0][1], unseen], axis=0),
                                  jnp.concatenate([unseen, halves[1][1]], axis=0)], axis=1)
            ctx = jnp.zeros((head_dim, tq), F32)
            for j, vr in enumerate(vt_refs):
                ctx = ctx + jnp.dot(vr[rows, :], pb[j * tq:(j + 1) * tq, :],
                                    preferred_element_type=F32)
            ctx_scr[rows, :] = (ctx / l).astype(BF16)
        out_t = jnp.dot(wot_ref[...], ctx_scr[...], preferred_element_type=F32)
        o_ref[...] = x_ref[...] + out_t.T

    pl.when(i < 2)(functools.partial(all_heads, True))
    pl.when(i >= 2)(functools.partial(all_heads, False))


def _attn_layer(x2d, q_t, k, v_t, bias_t, w_o_t, *, batch, tq=256):
    t, d = x2d.shape
    nt = t // batch // tq

    def cur(b, i):
        return b * nt + i

    def back(n):
        return lambda b, i: b * nt + jnp.maximum(i - n, 0)

    rows = lambda f: pl.BlockSpec((tq, d), lambda b, i: (f(b, i), 0))
    cols = lambda f: pl.BlockSpec((d, tq), lambda b, i: (0, f(b, i)))
    return pl.pallas_call(
        functools.partial(_attn_kernel, head_dim=d // ATT_HEADS),
        out_shape=jax.ShapeDtypeStruct((t, d), F32),
        grid=(batch, nt),
        in_specs=[rows(cur), cols(cur),
                  rows(back(2)), rows(back(1)), rows(cur),
                  cols(back(2)), cols(back(1)), cols(cur),
                  _resident(bias_t.shape), _resident(w_o_t.shape)],
        out_specs=rows(cur),
        scratch_shapes=[pltpu.VMEM((d, tq), BF16)],
        compiler_params=_params(("parallel", "arbitrary")),
        name="band_attention",
    )(x2d, q_t, k, k, k, v_t, v_t, v_t, bias_t, w_o_t)


def _band_bias(rel_table, tq):
    band = (ATT_LEFT_CHUNKS + 1) * CHUNK
    n_heads = rel_table.shape[0]
    assert 3 * tq >= band + tq - CHUNK and rel_table.shape[1] == REL_CLIP + CHUNK
    tail = jnp.broadcast_to(rel_table[:, -1:], (n_heads, band - 1 - REL_CLIP))
    rev = jnp.concatenate([rel_table, tail], axis=1).astype(F32)[:, ::-1]
    chunk = jnp.stack([rev[:, CHUNK - 1 - qi:CHUNK - 1 - qi + band] for qi in range(CHUNK)],
                      axis=1)
    lane_pad = -band % 128
    chunk = jnp.pad(chunk, ((0, 0), (0, 128 - CHUNK), (0, lane_pad)))
    return pl.pallas_call(
        functools.partial(_bias_tile_kernel, tq=tq, band=band),
        out_shape=jax.ShapeDtypeStruct((n_heads, 3 * tq, tq), F32),
        grid=(n_heads,),
        in_specs=[pl.BlockSpec((None, 128, band + lane_pad), lambda h: (h, 0, 0))],
        out_specs=pl.BlockSpec((None, 3 * tq, tq), lambda h: (h, 0, 0)),
        compiler_params=_params(("parallel",)),
        name="band_bias_tile",
    )(chunk)


def _bias_tile_kernel(chunk_ref, o_ref, *, tq, band):
    chunk_t = chunk_ref[...].T[:band, :CHUNK]
    cols = []
    for c in range(tq // CHUNK):
        parts = [chunk_t]
        if c > 0:
            parts.insert(0, jnp.full((c * CHUNK, CHUNK), NEG_INF, F32))
        below = 3 * tq - band - c * CHUNK
        if below > 0:
            parts.append(jnp.full((below, CHUNK), NEG_INF, F32))
        cols.append(jnp.concatenate(parts, axis=0))
    o_ref[...] = jnp.concatenate(cols, axis=1)


def _route_kernel(x_ref, g_ref, rw_ref, rb_ref, idx_ref, rank_ref, w_ref, cnt_ref, off_ref,
                  run_scr):
    tm = x_ref.shape[0]

    @pl.when(pl.program_id(0) == 0)
    def _():
        run_scr[...] = jnp.zeros_like(run_scr)

    off_ref[...] = run_scr[...].astype(jnp.int32)

    hf = _rmsnorm(x_ref[...], g_ref[...])
    n_e = rw_ref.shape[1]
    h_hi = hf.astype(BF16)
    h_lo = (hf - h_hi.astype(F32)).astype(BF16)
    rw = rw_ref[...]
    w_hi = rw.astype(BF16)
    w_lo = (rw - w_hi.astype(F32)).astype(BF16)
    by_hi = jnp.dot(h_hi, jnp.concatenate([w_hi, w_lo], axis=1), preferred_element_type=F32)
    logits = (by_hi[:, :n_e] + (by_hi[:, n_e:] + jnp.dot(h_lo, w_hi, preferred_element_type=F32))
              + rb_ref[...])
    lane = lax.broadcasted_iota(jnp.int32, logits.shape, 1)
    m1 = jnp.max(logits, axis=-1, keepdims=True)
    i1 = jnp.min(jnp.where(logits == m1, lane, n_e), axis=-1, keepdims=True)
    rest = jnp.where(lane == i1, -jnp.inf, logits)
    m2 = jnp.max(rest, axis=-1, keepdims=True)
    i2 = jnp.min(jnp.where(rest == m2, lane, n_e), axis=-1, keepdims=True)
    e2 = jnp.exp(m2 - m1)
    denom = 1.0 + e2

    sel = jnp.where((lane == i1) | (lane == i2), 1.0, 0.0)
    before = (lax.broadcasted_iota(jnp.int32, (tm, tm), 0)
              > lax.broadcasted_iota(jnp.int32, (tm, tm), 1))
    cum = jnp.dot(jnp.where(before, 1.0, 0.0).astype(BF16), sel.astype(BF16),
                  preferred_element_type=F32) + run_scr[...]
    r1 = jnp.sum(jnp.where(lane == i1, cum, 0.0), axis=-1, keepdims=True)
    r2 = jnp.sum(jnp.where(lane == i2, cum, 0.0), axis=-1, keepdims=True)
    run_scr[...] += jnp.sum(sel, axis=0, keepdims=True)

    idx_ref[...] = jnp.concatenate([i1, i2], axis=1)
    rank_ref[...] = jnp.concatenate([r1, r2], axis=1).astype(jnp.int32)
    w_ref[...] = jnp.concatenate([1.0 / denom, e2 / denom], axis=1)
    cnt_ref[...] = run_scr[...].astype(jnp.int32)


def _route(x2d, g, router_w, router_b, *, tm):
    t, d = x2d.shape
    n_e = router_w.shape[1]
    pair = pl.BlockSpec((tm, TOP_K), lambda i: (i, 0))
    return pl.pallas_call(
        _route_kernel,
        out_shape=(jax.ShapeDtypeStruct((t, TOP_K), jnp.int32),
                   jax.ShapeDtypeStruct((t, TOP_K), jnp.int32),
                   jax.ShapeDtypeStruct((t, TOP_K), F32),
                   jax.ShapeDtypeStruct((1, n_e), jnp.int32),
                   jax.ShapeDtypeStruct((t // tm, 1, n_e), jnp.int32)),
        grid=(t // tm,),
        in_specs=[pl.BlockSpec((tm, d), lambda i: (i, 0)), _resident((1, d)),
                  _resident(router_w.shape), _resident((1, n_e))],
        out_specs=(pair, pair, pair, pl.BlockSpec((1, n_e), lambda i: (0, 0)),
                   pl.BlockSpec((None, 1, n_e), lambda i: (i, 0, 0))),
        scratch_shapes=[pltpu.VMEM((1, n_e), F32)],
        compiler_params=_params(("arbitrary",)),
        name="moe_route",
    )(x2d, g, router_w, router_b)


def _wait_rows(hbm_ref, vmem_slot, sem, n_rows):
    pltpu.make_async_copy(hbm_ref.at[pl.ds(0, n_rows)], vmem_slot, sem).wait()


def _dispatch_kernel(lstart_ref, cnt_ref, gdst_ref, padst_ref, nused_ref, x_ref, g_ref, lpos_ref,
                     hs_ref, buf, zero_scr, sem, zero_sem, *, n_e, row_tile, min_used_tiles):
    i = pl.program_id(0)
    n = pl.num_programs(0)
    tb = x_ref.shape[0]
    n_loc = buf.shape[1]
    slot = lax.rem(i, 2)

    def run_copies(blk, s, act):
        for e in range(n_e):
            lstart = lstart_ref[blk * n_e + e]
            cnt = cnt_ref[blk * n_e + e]
            gdst = gdst_ref[blk * n_e + e]
            for log_len in range((tb + SUBLANES).bit_length() - 1, SUBLANES.bit_length() - 2, -1):
                length = 1 << log_len
                done = (cnt >> (log_len + 1)) << (log_len + 1)

                @pl.when((cnt & length) != 0)
                def _(lstart=lstart, gdst=gdst, done=done, length=length):
                    src = buf.at[s, pl.ds(pl.multiple_of(lstart + done, SUBLANES), length)]
                    dst = hs_ref.at[pl.ds(pl.multiple_of(gdst + done, SUBLANES), length)]
                    act(pltpu.make_async_copy(src, dst, sem.at[s]))

    @pl.when(i == 0)
    def _():
        zero_scr[...] = jnp.zeros_like(zero_scr)
        pads = [pltpu.make_async_copy(
            zero_scr, hs_ref.at[pl.ds(pl.multiple_of(padst_ref[e], SUBLANES), row_tile)], zero_sem)
            for e in range(n_e)]
        for cp in pads:
            cp.start()
        for cp in pads:
            cp.wait()
        for act in (lambda cp: cp.start(), lambda cp: cp.wait()):
            for j in range(min_used_tiles, hs_ref.shape[0] // row_tile):
                @pl.when(j >= nused_ref[0])
                def _(j=j, act=act):
                    act(pltpu.make_async_copy(
                        zero_scr, hs_ref.at[pl.ds(j * row_tile, row_tile)], zero_sem))

    @pl.when(i >= 2)
    def _():
        run_copies(i - 2, slot, lambda cp: cp.wait())

    h = _rmsnorm(x_ref[...], g_ref[...]).astype(BF16)
    row = lax.broadcasted_iota(jnp.int32, (n_loc, tb), 0)
    lpos = lpos_ref[...]
    picks = (row == lpos[0:1, :]) | (row == lpos[1:2, :])
    buf[slot] = jnp.dot(jnp.where(picks, 1.0, 0.0).astype(BF16), h, preferred_element_type=F32)
    run_copies(i, slot, lambda cp: cp.start())

    @pl.when(i == n - 1)
    def _():
        run_copies(i, slot, lambda cp: cp.wait())

    @pl.when((i == n - 1) & (n >= 2))
    def _():
        run_copies(i - 1, 1 - slot, lambda cp: cp.wait())


def _dispatch(lstart, cnt, gdst, pad_start, n_used, x2d, g, lpos_t, n_rows, *, tb, n_e, row_tile):
    t, d = x2d.shape
    return pl.pallas_call(
        functools.partial(_dispatch_kernel, n_e=n_e, row_tile=row_tile,
                          min_used_tiles=(TOP_K * t) // row_tile),
        out_shape=jax.ShapeDtypeStruct((n_rows + row_tile, d), F32),
        grid_spec=pltpu.PrefetchScalarGridSpec(
            num_scalar_prefetch=5,
            grid=(t // tb,),
            in_specs=[pl.BlockSpec((tb, d), lambda i, *_: (i, 0)),
                      pl.BlockSpec((1, d), lambda i, *_: (0, 0)),
                      pl.BlockSpec((TOP_K, tb), lambda i, *_: (0, i))],
            out_specs=pl.BlockSpec(memory_space=pl.ANY),
            scratch_shapes=[pltpu.VMEM((2, TOP_K * tb + SUBLANES * n_e, d), F32),
                            pltpu.VMEM((row_tile, d), F32),
                            pltpu.SemaphoreType.DMA((2,)), pltpu.SemaphoreType.DMA(())],
        ),
        compiler_params=_params(("arbitrary",)),
        name="moe_dispatch",
    )(lstart, cnt, gdst, pad_start, n_used, x2d, g, lpos_t)


def _experts_kernel(te_ref, nused_ref, hs_ref, wg_ref, wu_ref, wd_ref, y_ref, *, f_chunk):
    del te_ref
    i = pl.program_id(0)

    @pl.when(i < nused_ref[0])
    def _():
        h = hs_ref[...].astype(BF16)
        for c in range(0, wg_ref.shape[1], f_chunk):
            a = jnp.dot(h, wg_ref[:, c:c + f_chunk], preferred_element_type=F32)
            b = jnp.dot(h, wu_ref[:, c:c + f_chunk], preferred_element_type=F32)
            act = (jax.nn.silu(a) * b).astype(BF16)
            y = jnp.dot(act, wd_ref[c:c + f_chunk, :], preferred_element_type=F32)
            if c == 0:
                y_ref[...] = y
            else:
                y_ref[...] += y

    @pl.when(i >= nused_ref[0])
    def _():
        y_ref[...] = jnp.zeros_like(y_ref)


def _experts(tile_expert, n_used, hs, w_gate, w_up, w_down, *, tm, f_chunk=512):
    d = hs.shape[1]
    n_rows = tile_expert.shape[0] * tm
    f_dim = w_gate.shape[2]

    def row_blk(i, te, nu):
        return (jnp.maximum(jnp.minimum(i, nu[0] - 1), 0), 0)

    expert_blk = lambda i, te, nu: (te[i], 0, 0)
    return pl.pallas_call(
        functools.partial(_experts_kernel, f_chunk=f_chunk),
        out_shape=jax.ShapeDtypeStruct((n_rows, d), F32),
        grid_spec=pltpu.PrefetchScalarGridSpec(
            num_scalar_prefetch=2,
            grid=(n_rows // tm,),
            in_specs=[
                pl.BlockSpec((tm, d), row_blk),
                pl.BlockSpec((None, d, f_dim), expert_blk),
                pl.BlockSpec((None, d, f_dim), expert_blk),
                pl.BlockSpec((None, f_dim, d), expert_blk),
            ],
            out_specs=pl.BlockSpec((tm, d), lambda i, te, nu: (i, 0)),
        ),
        compiler_params=pltpu.CompilerParams(dimension_semantics=("arbitrary",),
                                             vmem_limit_bytes=EXPERTS_VMEM_LIMIT_BYTES),
        name="moe_experts",
    )(tile_expert, n_used, hs, w_gate, w_up, w_down)


def _combine_kernel(pos_ref, x_ref, w_ref, fg_ref, y_ref, o_ref, buf, sem):
    i = pl.program_id(0)
    n = pl.num_programs(0)
    tm = x_ref.shape[0]
    slot = lax.rem(i, 2)

    def gather(tile, s):
        base = tile * (TOP_K * tm)

        def issue(r, carry):
            for k in range(TOP_K):
                p = pos_ref[base + TOP_K * r + k]
                pltpu.make_async_copy(y_ref.at[pl.ds(p, 1)], buf.at[s, k, pl.ds(r, 1)],
                                      sem.at[s]).start()
            return carry

        lax.fori_loop(0, tm, issue, 0, unroll=8)

    @pl.when(i == 0)
    def _():
        gather(0, 0)

    @pl.when(i + 1 < n)
    def _():
        gather(i + 1, 1 - slot)

    for k in range(TOP_K):
        _wait_rows(y_ref, buf.at[slot, k], sem.at[slot], tm)

    w = w_ref[...]
    y = x_ref[...] + (w[:, 0:1] * buf[slot, 0] + w[:, 1:2] * buf[slot, 1])
    o_ref[...] = _rmsnorm(y, fg_ref[...])


def _combine(pos, x2d, w, final_g, y_sorted, *, tm=256):
    t, d = x2d.shape
    return pl.pallas_call(
        _combine_kernel,
        out_shape=jax.ShapeDtypeStruct((t, d), F32),
        grid_spec=pltpu.PrefetchScalarGridSpec(
            num_scalar_prefetch=1,
            grid=(t // tm,),
            in_specs=[pl.BlockSpec((tm, d), lambda i, pos: (i, 0)),
                      pl.BlockSpec((tm, TOP_K), lambda i, pos: (i, 0)),
                      pl.BlockSpec((1, d), lambda i, pos: (0, 0)),
                      pl.BlockSpec(memory_space=pl.ANY)],
            out_specs=pl.BlockSpec((tm, d), lambda i, pos: (i, 0)),
            scratch_shapes=[pltpu.VMEM((2, TOP_K, tm, d), F32), pltpu.SemaphoreType.DMA((2,))],
        ),
        compiler_params=_params(("arbitrary",)),
        name="moe_combine",
    )(pos, x2d, w, final_g, y_sorted)


def _moe_layer(x2d, g, router_w, router_b, w_gate, w_up, w_down, final_g, *,
               row_tile=512, tok_block=512):
    t, d = x2d.shape
    n_e = router_w.shape[1]
    n_blk = t // tok_block
    idx, rank, w, counts, blk_off = _route(x2d, g, router_w, router_b, tm=tok_block)

    counts = counts[0]
    blk_off = blk_off[:, 0, :]
    blk_cnt = jnp.concatenate([blk_off[1:], counts[None]], axis=0) - blk_off
    run_len = (blk_cnt + SUBLANES - 1) // SUBLANES * SUBLANES
    run_lstart = jnp.cumsum(run_len, axis=1) - run_len
    group_len = jnp.sum(run_len, axis=0)
    padded = (group_len + row_tile - 1) // row_tile * row_tile
    ends = jnp.cumsum(padded)
    starts = ends - padded
    run_gdst = starts[None, :] + jnp.cumsum(run_len, axis=0) - run_len

    experts = jnp.arange(n_e, dtype=jnp.int32)
    chosen_b = (idx[..., None] == experts).reshape(n_blk, tok_block, TOP_K, n_e)
    per_tok = lambda v: jnp.sum(jnp.where(chosen_b, v[:, None, None, :], 0), axis=-1)
    in_run = rank.reshape(n_blk, tok_block, TOP_K) - per_tok(blk_off)
    lpos_t = (in_run + per_tok(run_lstart)).reshape(t, TOP_K).T.astype(jnp.int32)
    pos = (in_run + per_tok(run_gdst)).reshape(-1).astype(jnp.int32)

    n_tiles = pl.cdiv(TOP_K * t + (SUBLANES - 1) * n_blk * n_e, row_tile) + n_e
    tile_ends = ends // row_tile
    n_used = tile_ends[-1:].astype(jnp.int32)
    tile_ids = jnp.arange(n_tiles, dtype=jnp.int32)
    tile_expert = jnp.sum(jnp.minimum(tile_ids, n_used - 1)[:, None] >= tile_ends[None, :],
                          axis=1).astype(jnp.int32)

    flat = lambda v: v.reshape(-1).astype(jnp.int32)
    hs = _dispatch(flat(run_lstart), flat(run_len), flat(run_gdst), flat(starts + group_len),
                   n_used, x2d, g, lpos_t, n_tiles * row_tile, tb=tok_block, n_e=n_e, row_tile=row_tile)
    y_sorted = _experts(tile_expert, n_used, hs, w_gate, w_up, w_down, tm=row_tile)
    return _combine(pos, x2d, w, final_g, y_sorted)


def kernel(x, norm_mix_g, norm_ffn_g, final_g, a_w_in, a_v_gain, a_w_s, a_b_s, a_w_out,
           a_ffn_gate, a_ffn_up, a_ffn_down, b_w_qkv, b_rel_bias, b_w_o, b_router_w,
           b_router_b, b_exp_gate, b_exp_up, b_exp_down):
    batch, seq, d = x.shape
    assert norm_mix_g.shape[0] == 2, "two layers: gMLP/SwiGLU then attention/MoE"
    head_dim = d // ATT_HEADS
    tq = 256
    xf = x.reshape(batch * seq, d)
    row = lambda vec: vec.reshape(1, -1).astype(F32)

    xf = _gmlp_layer(xf, row(norm_mix_g[0]), a_w_in[0], row(a_v_gain[0]),
                     a_w_s[0], a_b_s[0].T, a_w_out[0])
    xf = _swiglu_layer(xf, row(norm_ffn_g[0]), a_ffn_gate[0], a_ffn_up[0], a_ffn_down[0])

    w_qkv = b_w_qkv[0].astype(BF16)
    q_t, k, v_t = _qkv_layer(xf, row(norm_mix_g[1]), w_qkv[:, :d].T, w_qkv[:, d:2 * d],
                             w_qkv[:, 2 * d:].T, q_scale=head_dim ** -0.5)
    bias_t = _band_bias(b_rel_bias[0], tq)
    xf = _attn_layer(xf, q_t, k, v_t, bias_t, b_w_o[0].astype(BF16).T, batch=batch, tq=tq)
    xf = _moe_layer(xf, row(norm_ffn_g[1]), b_router_w[0], row(b_router_b[0]),
                    b_exp_gate[0].astype(BF16), b_exp_up[0].astype(BF16),
                    b_exp_down[0].astype(BF16), row(final_g))
    return xf.reshape(batch, seq, d)
```

```python
import functools

import jax
import jax.numpy as jnp
from jax import lax
from jax.experimental import pallas as pl
from jax.experimental.pallas import tpu as pltpu

EPS = 1e-6
CHUNK = 64
GMLP_BLOCK = 128
GMLP_GROUPS = 8
ATT_HEADS = 16
ATT_LEFT_CHUNKS = 8
REL_CLIP = 256
TOP_K = 2
NEG_INF = -1e30
SUBLANES = 8
LANES = 128

VMEM_LIMIT_BYTES = 56 * 1024 * 1024
EXPERTS_VMEM_LIMIT_BYTES = 60 * 1024 * 1024
WEIGHT_STAGE_BYTES = 2 * 1024 * 1024

F32 = jnp.float32
BF16 = jnp.bfloat16


def _params(semantics):
    return pltpu.CompilerParams(dimension_semantics=semantics,
                                vmem_limit_bytes=VMEM_LIMIT_BYTES)


def _resident(shape):
    zeros = (0,) * len(shape)
    return pl.BlockSpec(shape, lambda *_: zeros, pipeline_mode=pl.Buffered(1))


_HBM = pl.BlockSpec(memory_space=pl.ANY)


def _stage_rows(w):
    rows = w.shape[0]
    while rows * w.shape[1] * 4 > WEIGHT_STAGE_BYTES and rows % 2 == 0:
        rows //= 2
    return rows


def _stage_scratch(w):
    return [pltpu.VMEM(w.shape, BF16), pltpu.VMEM((2, _stage_rows(w), w.shape[1]), F32),
            pltpu.SemaphoreType.DMA((2,))]


def _stream_rows(w_hbm, stage, sem, consume):
    rows = stage.shape[1]
    n_chunks = w_hbm.shape[0] // rows
    chunk = lambda c: pltpu.make_async_copy(w_hbm.at[pl.ds(c * rows, rows)], stage.at[c % 2],
                                            sem.at[c % 2])
    chunk(0).start()
    for c in range(n_chunks):
        if c + 1 < n_chunks:
            chunk(c + 1).start()
        chunk(c).wait()
        consume(slice(c * rows, (c + 1) * rows), stage[c % 2])


def _load_weight_bf16(w_hbm, w_bf, stage, sem, *, transpose=False):
    def consume(rows, chunk):
        if transpose:
            w_bf[:, rows] = chunk.T.astype(BF16)
        else:
            w_bf[rows, :] = chunk.astype(BF16)

    _stream_rows(w_hbm, stage, sem, consume)


def _rmsnorm(x, g):
    ms = jnp.mean(x * x, axis=-1, keepdims=True)
    return x * lax.rsqrt(ms + EPS) * g


def _gelu_exact(z):
    return 0.5 * z * (1.0 + lax.erf(z * (2.0 ** -0.5)))


def _gmlp_kernel(x_ref, g_ref, win_hbm, vgain_ref, ws_ref, bst_ref, wout_hbm,
                 o_ref, u_scr, v_scr, gated_scr, win_ref, win_stage, win_sem,
                 wout_ref, wout_stage, wout_sem, *, n_chunk):
    tm = x_ref.shape[0]
    half = u_scr.shape[1]
    group_dim = half // GMLP_GROUPS

    @pl.when(pl.program_id(0) == 0)
    def _():
        _load_weight_bf16(win_hbm, win_ref, win_stage, win_sem)
        _load_weight_bf16(wout_hbm, wout_ref, wout_stage, wout_sem)

    x = x_ref[...]
    h = _rmsnorm(x, g_ref[...]).astype(BF16)

    ssq = jnp.zeros((tm, 1), F32)
    for c in range(0, 2 * half, n_chunk):
        z = jnp.dot(h, win_ref[:, c:c + n_chunk], preferred_element_type=F32)
        z = _gelu_exact(z)
        if c < half:
            u_scr[:, c:c + n_chunk] = z
        else:
            v_scr[:, c - half:c - half + n_chunk] = z
            ssq = ssq + jnp.sum(z * z, axis=-1, keepdims=True)
    inv = lax.rsqrt(ssq * (1.0 / half) + EPS)

    row_chunk = lax.broadcasted_iota(jnp.int32, (GMLP_BLOCK, GMLP_BLOCK), 0) // CHUNK
    col_chunk = lax.broadcasted_iota(jnp.int32, (GMLP_BLOCK, GMLP_BLOCK), 1) // CHUNK
    causal = row_chunk >= col_chunk
    for g in range(GMLP_GROUPS):
        cols = slice(g * group_dim, (g + 1) * group_dim)
        w_g = jnp.where(causal, ws_ref[g], 0.0).astype(BF16)
        b_g = bst_ref[:, g:g + 1]
        for n in range(tm // GMLP_BLOCK):
            rows = slice(n * GMLP_BLOCK, (n + 1) * GMLP_BLOCK)
            vn = (v_scr[rows, cols] * inv[rows] * vgain_ref[:, cols]).astype(BF16)
            sv = jnp.dot(w_g, vn, preferred_element_type=F32) + b_g
            gated_scr[rows, cols] = (u_scr[rows, cols] * sv).astype(BF16)

    out = jnp.dot(gated_scr[...], wout_ref[...], preferred_element_type=F32)
    o_ref[...] = x + out


def _gmlp_layer(x2d, g, w_in, v_gain, w_s, b_s_t, w_out, *, tm=512, n_chunk=512):
    t, d = x2d.shape
    half = w_out.shape[0]
    return pl.pallas_call(
        functools.partial(_gmlp_kernel, n_chunk=n_chunk),
        out_shape=jax.ShapeDtypeStruct((t, d), F32),
        grid=(t // tm,),
        in_specs=[
            pl.BlockSpec((tm, d), lambda i: (i, 0)),
            _resident((1, d)),
            _HBM,
            _resident((1, half)),
            _resident(w_s.shape),
            _resident(b_s_t.shape),
            _HBM,
        ],
        out_specs=pl.BlockSpec((tm, d), lambda i: (i, 0)),
        scratch_shapes=[
            pltpu.VMEM((tm, half), F32),
            pltpu.VMEM((tm, half), F32),
            pltpu.VMEM((tm, half), BF16),
            *_stage_scratch(w_in), *_stage_scratch(w_out),
        ],
        compiler_params=_params(("arbitrary",)),
        name="gmlp_mixer",
    )(x2d, g, w_in, v_gain, w_s, b_s_t, w_out)


def _swiglu_kernel(x_ref, g_ref, wg_hbm, wu_hbm, wd_hbm, o_ref, wg_ref, wg_stage, wg_sem,
                   wu_ref, wu_stage, wu_sem, wd_ref, wd_stage, wd_sem, *, f_chunk):
    @pl.when(pl.program_id(0) == 0)
    def _():
        _load_weight_bf16(wg_hbm, wg_ref, wg_stage, wg_sem)
        _load_weight_bf16(wu_hbm, wu_ref, wu_stage, wu_sem)
        _load_weight_bf16(wd_hbm, wd_ref, wd_stage, wd_sem)

    x = x_ref[...]
    h = _rmsnorm(x, g_ref[...]).astype(BF16)
    o_ref[...] = x
    for c in range(0, wg_ref.shape[1], f_chunk):
        a = jnp.dot(h, wg_ref[:, c:c + f_chunk], preferred_element_type=F32)
        b = jnp.dot(h, wu_ref[:, c:c + f_chunk], preferred_element_type=F32)
        act = (jax.nn.silu(a) * b).astype(BF16)
        o_ref[...] += jnp.dot(act, wd_ref[c:c + f_chunk, :], preferred_element_type=F32)


def _swiglu_layer(x2d, g, w_gate, w_up, w_down, *, tm=512, f_chunk=256):
    t, d = x2d.shape
    return pl.pallas_call(
        functools.partial(_swiglu_kernel, f_chunk=f_chunk),
        out_shape=jax.ShapeDtypeStruct((t, d), F32),
        grid=(t // tm,),
        in_specs=[pl.BlockSpec((tm, d), lambda i: (i, 0)), _resident((1, d)), _HBM, _HBM, _HBM],
        out_specs=pl.BlockSpec((tm, d), lambda i: (i, 0)),
        scratch_shapes=[*_stage_scratch(w_gate), *_stage_scratch(w_up), *_stage_scratch(w_down)],
        compiler_params=_params(("arbitrary",)),
        name="dense_swiglu",
    )(x2d, g, w_gate, w_up, w_down)


def _qkv_kernel(x_ref, g_ref, w_hbm, qt_ref, k_ref, vt_ref, wqt_ref, wk_ref, wvt_ref,
                stage, sem, *, q_scale):
    d = x_ref.shape[1]

    @pl.when(pl.program_id(0) == 0)
    def _():
        def consume(rows, chunk):
            wqt_ref[:, rows] = chunk[:, 0:d].T.astype(BF16)
            wk_ref[rows, :] = chunk[:, d:2 * d].astype(BF16)
            wvt_ref[:, rows] = chunk[:, 2 * d:3 * d].T.astype(BF16)

        _stream_rows(w_hbm, stage, sem, consume)

    h = _rmsnorm(x_ref[...], g_ref[...]).astype(BF16)
    nt_dims = (((1,), (1,)), ((), ()))
    qt = lax.dot_general(wqt_ref[...], h, nt_dims, preferred_element_type=F32)
    qt_ref[...] = (qt * q_scale).astype(BF16)
    k_ref[...] = jnp.dot(h, wk_ref[...], preferred_element_type=F32).astype(BF16)
    vt_ref[...] = lax.dot_general(wvt_ref[...], h, nt_dims,
                                  preferred_element_type=F32).astype(BF16)


def _qkv_layer(x2d, g, w_qkv, *, q_scale, tm=512):
    t, d = x2d.shape
    row_spec = pl.BlockSpec((tm, d), lambda i: (i, 0))
    col_spec = pl.BlockSpec((d, tm), lambda i: (0, i))
    return pl.pallas_call(
        functools.partial(_qkv_kernel, q_scale=q_scale),
        out_shape=(jax.ShapeDtypeStruct((d, t), BF16), jax.ShapeDtypeStruct((t, d), BF16),
                   jax.ShapeDtypeStruct((d, t), BF16)),
        grid=(t // tm,),
        in_specs=[row_spec, _resident((1, d)), _HBM],
        out_specs=(col_spec, row_spec, col_spec),
        scratch_shapes=[pltpu.VMEM((d, d), BF16), pltpu.VMEM((d, d), BF16),
                        pltpu.VMEM((d, d), BF16),
                        pltpu.VMEM((2, _stage_rows(w_qkv), 3 * d), F32),
                        pltpu.SemaphoreType.DMA((2,))],
        compiler_params=_params(("arbitrary",)),
        name="qkv_proj",
    )(x2d, g, w_qkv)


def _attn_kernel(x_ref, qt_ref, k0_ref, k1_ref, k2_ref, vt0_ref, vt1_ref, vt2_ref,
                 bias_ref, wo_hbm, o_ref, ctx_scr, wot_ref, wo_stage, wo_sem, *, head_dim):
    tq = qt_ref.shape[1]
    i = pl.program_id(1)

    @pl.when((pl.program_id(0) == 0) & (i == 0))
    def _():
        _load_weight_bf16(wo_hbm, wot_ref, wo_stage, wo_sem, transpose=True)

    k_refs = (k0_ref, k1_ref, k2_ref)
    vt_refs = (vt0_ref, vt1_ref, vt2_ref)
    n_win = len(k_refs) * tq
    half_q = tq // 2
    assert (ATT_LEFT_CHUNKS + 1) * CHUNK + half_q - CHUNK <= n_win - half_q
    key_pos = lax.broadcasted_iota(jnp.int32, (n_win, 1), 0) + (i - 2) * tq
    in_seq = key_pos >= 0
    zero_half = jnp.zeros((head_dim, tq), BF16)

    def scores(h):
        pair = slice((h // 2) * 2 * head_dim, (h // 2 + 1) * 2 * head_dim)
        qt_h = qt_ref[h * head_dim:(h + 1) * head_dim, :]
        rhs = jnp.concatenate([qt_h, zero_half] if h % 2 == 0 else [zero_half, qt_h], axis=0)
        return jnp.concatenate(
            [jnp.dot(kr[:, pair], rhs, preferred_element_type=F32) for kr in k_refs],
            axis=0)

    def all_heads(window_has_padding):
        s_next = scores(0)
        for h in range(ATT_HEADS):
            rows = slice(h * head_dim, (h + 1) * head_dim)
            s = s_next
            if h + 1 < ATT_HEADS:
                s_next = scores(h + 1)
            halves = []
            for lanes, keys in ((slice(0, half_q), slice(0, n_win - half_q)),
                                (slice(half_q, tq), slice(half_q, n_win))):
                sh = s[keys, lanes] + bias_ref[h, keys, lanes]
                if window_has_padding:
                    sh = jnp.where(in_seq[keys], sh, NEG_INF)
                m = jnp.max(sh, axis=0, keepdims=True)
                p = jnp.exp(sh - m)
                halves.append((jnp.sum(p, axis=0, keepdims=True), p.astype(BF16)))
            unseen = jnp.zeros((half_q, half_q), BF16)
            l = jnp.concatenate([halves[0][0], halves[1][0]], axis=1)
            pb = jnp.concatenate([jnp.concatenate([halves[0][1], unseen], axis=0),
                                  jnp.concatenate([unseen, halves[1][1]], axis=0)], axis=1)
            ctx = jnp.zeros((head_dim, tq), F32)
            for j, vr in enumerate(vt_refs):
                ctx = ctx + jnp.dot(vr[rows, :], pb[j * tq:(j + 1) * tq, :],
                                    preferred_element_type=F32)
            ctx_scr[rows, :] = (ctx / l).astype(BF16)
        out_t = jnp.dot(wot_ref[...], ctx_scr[...], preferred_element_type=F32)
        o_ref[...] = x_ref[...] + out_t.T

    pl.when(i < 2)(functools.partial(all_heads, True))
    pl.when(i >= 2)(functools.partial(all_heads, False))


def _attn_layer(x2d, q_t, k, v_t, bias_t, w_o, *, batch, tq=256):
    t, d = x2d.shape
    nt = t // batch // tq

    def cur(b, i):
        return b * nt + i

    def back(n):
        return lambda b, i: b * nt + jnp.maximum(i - n, 0)

    rows = lambda f: pl.BlockSpec((tq, d), lambda b, i: (f(b, i), 0))
    cols = lambda f: pl.BlockSpec((d, tq), lambda b, i: (0, f(b, i)))
    return pl.pallas_call(
        functools.partial(_attn_kernel, head_dim=d // ATT_HEADS),
        out_shape=jax.ShapeDtypeStruct((t, d), F32),
        grid=(batch, nt),
        in_specs=[rows(cur), cols(cur),
                  rows(back(2)), rows(back(1)), rows(cur),
                  cols(back(2)), cols(back(1)), cols(cur),
                  _resident(bias_t.shape), _HBM],
        out_specs=rows(cur),
        scratch_shapes=[pltpu.VMEM((d, tq), BF16), *_stage_scratch(w_o)],
        compiler_params=_params(("arbitrary", "arbitrary")),
        name="band_attention",
    )(x2d, q_t, k, k, k, v_t, v_t, v_t, bias_t, w_o)


def _band_bias(rel_table, tq):
    band = (ATT_LEFT_CHUNKS + 1) * CHUNK
    n_heads = rel_table.shape[0]
    assert 3 * tq >= band + tq - CHUNK and rel_table.shape[1] == REL_CLIP + CHUNK
    tail = jnp.broadcast_to(rel_table[:, -1:], (n_heads, band - 1 - REL_CLIP))
    rev = jnp.concatenate([rel_table, tail], axis=1).astype(F32)[:, ::-1]
    chunk = jnp.stack([rev[:, CHUNK - 1 - qi:CHUNK - 1 - qi + band] for qi in range(CHUNK)],
                      axis=1)
    lane_pad = -band % 128
    chunk = jnp.pad(chunk, ((0, 0), (0, 128 - CHUNK), (0, lane_pad)))
    return pl.pallas_call(
        functools.partial(_bias_tile_kernel, tq=tq, band=band),
        out_shape=jax.ShapeDtypeStruct((n_heads, 3 * tq, tq), F32),
        grid=(n_heads,),
        in_specs=[pl.BlockSpec((None, 128, band + lane_pad), lambda h: (h, 0, 0))],
        out_specs=pl.BlockSpec((None, 3 * tq, tq), lambda h: (h, 0, 0)),
        compiler_params=_params(("parallel",)),
        name="band_bias_tile",
    )(chunk)


def _bias_tile_kernel(chunk_ref, o_ref, *, tq, band):
    chunk_t = chunk_ref[...].T[:band, :CHUNK]
    cols = []
    for c in range(tq // CHUNK):
        parts = [chunk_t]
        if c > 0:
            parts.insert(0, jnp.full((c * CHUNK, CHUNK), NEG_INF, F32))
        below = 3 * tq - band - c * CHUNK
        if below > 0:
            parts.append(jnp.full((below, CHUNK), NEG_INF, F32))
        cols.append(jnp.concatenate(parts, axis=0))
    o_ref[...] = jnp.concatenate(cols, axis=1)


def _route_kernel(x_ref, g_ref, rw_ref, rb_ref, idx_ref, rank_ref, w_ref, cnt_ref, off_ref,
                  run_scr):
    tm = x_ref.shape[0]

    @pl.when(pl.program_id(0) == 0)
    def _():
        run_scr[...] = jnp.zeros_like(run_scr)

    off_ref[...] = run_scr[...].astype(jnp.int32)

    hf = _rmsnorm(x_ref[...], g_ref[...])
    n_e = rw_ref.shape[1]
    h_hi = hf.astype(BF16)
    h_lo = (hf - h_hi.astype(F32)).astype(BF16)
    rw = rw_ref[...]
    w_hi = rw.astype(BF16)
    w_lo = (rw - w_hi.astype(F32)).astype(BF16)
    by_hi = jnp.dot(h_hi, jnp.concatenate([w_hi, w_lo], axis=1), preferred_element_type=F32)
    logits = (by_hi[:, :n_e] + (by_hi[:, n_e:] + jnp.dot(h_lo, w_hi, preferred_element_type=F32))
              + rb_ref[...])
    lane = lax.broadcasted_iota(jnp.int32, logits.shape, 1)
    m1 = jnp.max(logits, axis=-1, keepdims=True)
    i1 = jnp.min(jnp.where(logits == m1, lane, n_e), axis=-1, keepdims=True)
    rest = jnp.where(lane == i1, -jnp.inf, logits)
    m2 = jnp.max(rest, axis=-1, keepdims=True)
    i2 = jnp.min(jnp.where(rest == m2, lane, n_e), axis=-1, keepdims=True)
    e2 = jnp.exp(m2 - m1)
    denom = 1.0 + e2

    sel = jnp.where((lane == i1) | (lane == i2), 1.0, 0.0)
    before = (lax.broadcasted_iota(jnp.int32, (tm, tm), 0)
              > lax.broadcasted_iota(jnp.int32, (tm, tm), 1))
    cum = jnp.dot(jnp.where(before, 1.0, 0.0).astype(BF16), sel.astype(BF16),
                  preferred_element_type=F32) + run_scr[...]
    r1 = jnp.sum(jnp.where(lane == i1, cum, 0.0), axis=-1, keepdims=True)
    r2 = jnp.sum(jnp.where(lane == i2, cum, 0.0), axis=-1, keepdims=True)
    run_scr[...] += jnp.sum(sel, axis=0, keepdims=True)

    idx_ref[...] = jnp.concatenate([i1, i2], axis=1)
    rank_ref[...] = jnp.concatenate([r1, r2], axis=1).astype(jnp.int32)
    w_ref[...] = jnp.concatenate([1.0 / denom, e2 / denom], axis=1)
    cnt_ref[...] = run_scr[...].astype(jnp.int32)


def _route(x2d, g, router_w, router_b, *, tm):
    t, d = x2d.shape
    n_e = router_w.shape[1]
    pair = pl.BlockSpec((tm, TOP_K), lambda i: (i, 0))
    return pl.pallas_call(
        _route_kernel,
        out_shape=(jax.ShapeDtypeStruct((t, TOP_K), jnp.int32),
                   jax.ShapeDtypeStruct((t, TOP_K), jnp.int32),
                   jax.ShapeDtypeStruct((t, TOP_K), F32),
                   jax.ShapeDtypeStruct((1, n_e), jnp.int32),
                   jax.ShapeDtypeStruct((t // tm, 1, n_e), jnp.int32)),
        grid=(t // tm,),
        in_specs=[pl.BlockSpec((tm, d), lambda i: (i, 0)), _resident((1, d)),
                  _resident(router_w.shape), _resident((1, n_e))],
        out_specs=(pair, pair, pair, pl.BlockSpec((1, n_e), lambda i: (0, 0)),
                   pl.BlockSpec((None, 1, n_e), lambda i: (i, 0, 0))),
        scratch_shapes=[pltpu.VMEM((1, n_e), F32)],
        compiler_params=_params(("arbitrary",)),
        name="moe_route",
    )(x2d, g, router_w, router_b)


def _run_pieces(lstart_ref, len_ref, sstart_ref, blk, *, n_e, max_run, visit):
    for e in range(n_e):
        lstart = lstart_ref[blk * n_e + e]
        run_len = len_ref[blk * n_e + e]
        sstart = sstart_ref[blk * n_e + e]
        for log_len in range(max_run.bit_length() - 1, SUBLANES.bit_length() - 2, -1):
            length = 1 << log_len
            done = (run_len >> (log_len + 1)) << (log_len + 1)

            @pl.when((run_len & length) != 0)
            def _(lstart=lstart, sstart=sstart, done=done, length=length):
                visit(pl.multiple_of(lstart + done, SUBLANES),
                      pl.multiple_of(sstart + done, SUBLANES), length)


def _dispatch_kernel(lstart_ref, cnt_ref, gdst_ref, padst_ref, nused_ref, x_ref, g_ref, lpos_ref,
                     w_ref, hs_ref, buf, zero_scr, sem, zero_sem, *, n_e, row_tile,
                     min_used_tiles):
    i = pl.program_id(0)
    n = pl.num_programs(0)
    tb = x_ref.shape[0]
    n_loc = buf.shape[1]
    slot = lax.rem(i, 2)

    def run_copies(blk, s, act):
        _run_pieces(lstart_ref, cnt_ref, gdst_ref, blk, n_e=n_e, max_run=tb + SUBLANES,
                    visit=lambda lo, so, length: act(pltpu.make_async_copy(
                        buf.at[s, pl.ds(lo, length)], hs_ref.at[pl.ds(so, length)], sem.at[s])))

    @pl.when(i == 0)
    def _():
        zero_scr[...] = jnp.zeros_like(zero_scr)
        pads = [pltpu.make_async_copy(
            zero_scr, hs_ref.at[pl.ds(pl.multiple_of(padst_ref[e], SUBLANES), row_tile)], zero_sem)
            for e in range(n_e)]
        for cp in pads:
            cp.start()
        for cp in pads:
            cp.wait()
        for act in (lambda cp: cp.start(), lambda cp: cp.wait()):
            for j in range(min_used_tiles, hs_ref.shape[0] // row_tile):
                @pl.when(j >= nused_ref[0])
                def _(j=j, act=act):
                    act(pltpu.make_async_copy(
                        zero_scr, hs_ref.at[pl.ds(j * row_tile, row_tile)], zero_sem))

    @pl.when(i >= 2)
    def _():
        run_copies(i - 2, slot, lambda cp: cp.wait())

    d = x_ref.shape[1]
    h = _rmsnorm(x_ref[...], g_ref[...]).astype(BF16)
    row = lax.broadcasted_iota(jnp.int32, (n_loc, tb), 0)
    lpos = lpos_ref[...]
    one_hot = lambda picks: jnp.where(picks, 1.0, 0.0).astype(BF16)
    picks = [row == lpos[k:k + 1, :] for k in range(TOP_K)]
    buf[slot, :, 0:d] = jnp.dot(one_hot(picks[0] | picks[1]), h, preferred_element_type=F32)
    lane = lax.broadcasted_iota(jnp.int32, (tb, LANES), 1)
    gates = jnp.zeros((n_loc, LANES), F32)
    for k in range(TOP_K):
        rest = w_ref[:, k:k + 1]
        pieces = jnp.zeros((tb, LANES), F32)
        for j in range(3):
            piece = rest.astype(BF16).astype(F32)
            rest = rest - piece
            pieces = jnp.where(lane == j, piece, pieces)
        gates = gates + jnp.dot(one_hot(picks[k]), pieces.astype(BF16),
                                preferred_element_type=F32)
    buf[slot, :, d:d + LANES] = gates
    run_copies(i, slot, lambda cp: cp.start())

    @pl.when(i == n - 1)
    def _():
        run_copies(i, slot, lambda cp: cp.wait())

    @pl.when((i == n - 1) & (n >= 2))
    def _():
        run_copies(i - 1, 1 - slot, lambda cp: cp.wait())


def _dispatch(lstart, cnt, gdst, pad_start, n_used, x2d, g, lpos_t, w, n_rows, *, tb, n_e,
              row_tile):
    t, d = x2d.shape
    width = d + LANES
    return pl.pallas_call(
        functools.partial(_dispatch_kernel, n_e=n_e, row_tile=row_tile,
                          min_used_tiles=(TOP_K * t) // row_tile),
        out_shape=jax.ShapeDtypeStruct((n_rows + row_tile, width), F32),
        grid_spec=pltpu.PrefetchScalarGridSpec(
            num_scalar_prefetch=5,
            grid=(t // tb,),
            in_specs=[pl.BlockSpec((tb, d), lambda i, *_: (i, 0)),
                      pl.BlockSpec((1, d), lambda i, *_: (0, 0)),
                      pl.BlockSpec((TOP_K, tb), lambda i, *_: (0, i)),
                      pl.BlockSpec((tb, TOP_K), lambda i, *_: (i, 0))],
            out_specs=pl.BlockSpec(memory_space=pl.ANY),
            scratch_shapes=[pltpu.VMEM((2, _block_rows(tb, n_e), width), F32),
                            pltpu.VMEM((row_tile, width), F32),
                            pltpu.SemaphoreType.DMA((2,)), pltpu.SemaphoreType.DMA(())],
        ),
        compiler_params=_params(("arbitrary",)),
        name="moe_dispatch",
    )(lstart, cnt, gdst, pad_start, n_used, x2d, g, lpos_t, w)


def _block_rows(tb, n_e):
    return TOP_K * tb + SUBLANES * n_e


def _experts_kernel(te_ref, nused_ref, hs_ref, wg_ref, wu_ref, wd_ref, y_ref, *, f_chunk):
    del te_ref
    i = pl.program_id(0)

    @pl.when(i < nused_ref[0])
    def _():
        d = y_ref.shape[1]
        f_dim = wg_ref.shape[1]
        h = hs_ref[:, 0:d].astype(BF16)
        gate = jnp.sum(hs_ref[:, d:], axis=-1, keepdims=True)
        for c in range(0, f_dim, f_chunk):
            a = jnp.dot(h, wg_ref[:, c:c + f_chunk], preferred_element_type=F32)
            b = jnp.dot(h, wu_ref[:, c:c + f_chunk], preferred_element_type=F32)
            act = (jax.nn.silu(a) * b).astype(BF16)
            y = jnp.dot(act, wd_ref[c:c + f_chunk, :], preferred_element_type=F32)
            if c == 0:
                y_ref[...] = y
            elif c + f_chunk < f_dim:
                y_ref[...] += y
            else:
                y_ref[...] = gate * (y_ref[...] + y)

    @pl.when(i >= nused_ref[0])
    def _():
        y_ref[...] = jnp.zeros_like(y_ref)


def _experts(tile_expert, n_used, hs, w_gate, w_up, w_down, *, tm, f_chunk=512):
    d = w_gate.shape[1]
    n_rows = tile_expert.shape[0] * tm
    f_dim = w_gate.shape[2]

    def row_blk(i, te, nu):
        return (jnp.maximum(jnp.minimum(i, nu[0] - 1), 0), 0)

    expert_blk = lambda i, te, nu: (te[i], 0, 0)
    return pl.pallas_call(
        functools.partial(_experts_kernel, f_chunk=f_chunk),
        out_shape=jax.ShapeDtypeStruct((n_rows, d), F32),
        grid_spec=pltpu.PrefetchScalarGridSpec(
            num_scalar_prefetch=2,
            grid=(n_rows // tm,),
            in_specs=[
                pl.BlockSpec((tm, hs.shape[1]), row_blk),
                pl.BlockSpec((None, d, f_dim), expert_blk),
                pl.BlockSpec((None, d, f_dim), expert_blk),
                pl.BlockSpec((None, f_dim, d), expert_blk),
            ],
            out_specs=pl.BlockSpec((tm, d), lambda i, te, nu: (i, 0)),
        ),
        compiler_params=pltpu.CompilerParams(dimension_semantics=("arbitrary",),
                                             vmem_limit_bytes=EXPERTS_VMEM_LIMIT_BYTES),
        name="moe_experts",
    )(tile_expert, n_used, hs, w_gate, w_up, w_down)


def _combine_kernel(lstart_ref, cnt_ref, gsrc_ref, x_ref, lpos_t_ref, fg_ref, y_ref, o_ref,
                    buf, sem, *, n_e):
    i = pl.program_id(0)
    n = pl.num_programs(0)
    tb = x_ref.shape[0]
    n_loc = buf.shape[1]
    slot = lax.rem(i, 2)

    def run_copies(blk, s, act):
        _run_pieces(lstart_ref, cnt_ref, gsrc_ref, blk, n_e=n_e, max_run=tb + SUBLANES,
                    visit=lambda lo, so, length: act(pltpu.make_async_copy(
                        y_ref.at[pl.ds(so, length)], buf.at[s, pl.ds(lo, length)], sem.at[s])))

    def fetch(blk, s):
        buf[s] = jnp.zeros(buf.shape[1:], F32)
        run_copies(blk, s, lambda cp: cp.start())

    @pl.when(i == 0)
    def _():
        fetch(0, 0)

    @pl.when(i + 1 < n)
    def _():
        fetch(i + 1, 1 - slot)

    run_copies(i, slot, lambda cp: cp.wait())

    y = buf[slot]
    y_hi = y.astype(BF16)
    y_lo = (y - y_hi.astype(F32)).astype(BF16)
    col = lax.broadcasted_iota(jnp.int32, (tb, n_loc), 1)
    lpos_t = lpos_t_ref[...]
    picks = (col == lpos_t[:, 0:1]) | (col == lpos_t[:, 1:2])
    one_hot = jnp.where(picks, 1.0, 0.0).astype(BF16)
    moe = (jnp.dot(one_hot, y_hi, preferred_element_type=F32)
           + jnp.dot(one_hot, y_lo, preferred_element_type=F32))
    o_ref[...] = _rmsnorm(x_ref[...] + moe, fg_ref[...])


def _combine(lstart, cnt, gsrc, x2d, lpos, final_g, y_sorted, *, tb, n_e):
    t, d = x2d.shape
    return pl.pallas_call(
        functools.partial(_combine_kernel, n_e=n_e),
        out_shape=jax.ShapeDtypeStruct((t, d), F32),
        grid_spec=pltpu.PrefetchScalarGridSpec(
            num_scalar_prefetch=3,
            grid=(t // tb,),
            in_specs=[pl.BlockSpec((tb, d), lambda i, *_: (i, 0)),
                      pl.BlockSpec((tb, TOP_K), lambda i, *_: (i, 0)),
                      pl.BlockSpec((1, d), lambda i, *_: (0, 0)),
                      pl.BlockSpec(memory_space=pl.ANY)],
            out_specs=pl.BlockSpec((tb, d), lambda i, *_: (i, 0)),
            scratch_shapes=[pltpu.VMEM((2, _block_rows(tb, n_e), d), F32),
                            pltpu.SemaphoreType.DMA((2,))],
        ),
        compiler_params=_params(("arbitrary",)),
        name="moe_combine",
    )(lstart, cnt, gsrc, x2d, lpos, final_g, y_sorted)


def _moe_layer(x2d, g, router_w, router_b, w_gate, w_up, w_down, final_g, *,
               row_tile=512, tok_block=512):
    t, d = x2d.shape
    n_e = router_w.shape[1]
    n_blk = t // tok_block
    idx, rank, w, counts, blk_off = _route(x2d, g, router_w, router_b, tm=tok_block)

    counts = counts[0]
    blk_off = blk_off[:, 0, :]
    blk_cnt = jnp.concatenate([blk_off[1:], counts[None]], axis=0) - blk_off
    run_len = (blk_cnt + SUBLANES - 1) // SUBLANES * SUBLANES
    run_lstart = jnp.cumsum(run_len, axis=1) - run_len
    group_len = jnp.sum(run_len, axis=0)
    padded = (group_len + row_tile - 1) // row_tile * row_tile
    ends = jnp.cumsum(padded)
    starts = ends - padded
    run_gdst = starts[None, :] + jnp.cumsum(run_len, axis=0) - run_len

    experts = jnp.arange(n_e, dtype=jnp.int32)
    chosen_b = (idx[..., None] == experts).reshape(n_blk, tok_block, TOP_K, n_e)
    per_tok = lambda v: jnp.sum(jnp.where(chosen_b, v[:, None, None, :], 0), axis=-1)
    in_run = rank.reshape(n_blk, tok_block, TOP_K) - per_tok(blk_off)
    lpos = (in_run + per_tok(run_lstart)).reshape(t, TOP_K).astype(jnp.int32)

    n_tiles = pl.cdiv(TOP_K * t + (SUBLANES - 1) * n_blk * n_e, row_tile) + n_e
    tile_ends = ends // row_tile
    n_used = tile_ends[-1:].astype(jnp.int32)
    tile_ids = jnp.arange(n_tiles, dtype=jnp.int32)
    tile_expert = jnp.sum(jnp.minimum(tile_ids, n_used - 1)[:, None] >= tile_ends[None, :],
                          axis=1).astype(jnp.int32)

    flat = lambda v: v.reshape(-1).astype(jnp.int32)
    runs = (flat(run_lstart), flat(run_len), flat(run_gdst))
    hs = _dispatch(*runs, flat(starts + group_len), n_used, x2d, g, lpos.T, w,
                   n_tiles * row_tile, tb=tok_block, n_e=n_e, row_tile=row_tile)
    y_sorted = _experts(tile_expert, n_used, hs, w_gate, w_up, w_down, tm=row_tile)
    return _combine(*runs, x2d, lpos, final_g, y_sorted, tb=tok_block, n_e=n_e)


def kernel(x, norm_mix_g, norm_ffn_g, final_g, a_w_in, a_v_gain, a_w_s, a_b_s, a_w_out,
           a_ffn_gate, a_ffn_up, a_ffn_down, b_w_qkv, b_rel_bias, b_w_o, b_router_w,
           b_router_b, b_exp_gate, b_exp_up, b_exp_down):
    batch, seq, d = x.shape
    assert norm_mix_g.shape[0] == 2, "two layers: gMLP/SwiGLU then attention/MoE"
    head_dim = d // ATT_HEADS
    tq = 256
    xf = x.reshape(batch * seq, d)
    row = lambda vec: vec.reshape(1, -1).astype(F32)

    xf = _gmlp_layer(xf, row(norm_mix_g[0]), a_w_in[0], row(a_v_gain[0]),
                     a_w_s[0], a_b_s[0].T, a_w_out[0])
    xf = _swiglu_layer(xf, row(norm_ffn_g[0]), a_ffn_gate[0], a_ffn_up[0], a_ffn_down[0])

    q_t, k, v_t = _qkv_layer(xf, row(norm_mix_g[1]), b_w_qkv[0], q_scale=head_dim ** -0.5)
    bias_t = _band_bias(b_rel_bias[0], tq)
    xf = _attn_layer(xf, q_t, k, v_t, bias_t, b_w_o[0], batch=batch, tq=tq)
    xf = _moe_layer(xf, row(norm_ffn_g[1]), b_router_w[0], row(b_router_b[0]),
                    b_exp_gate[0].astype(BF16), b_exp_up[0].astype(BF16),
                    b_exp_down[0].astype(BF16), row(final_g))
    return xf.reshape(batch, seq, d)
```

```python
import functools

import jax
import jax.numpy as jnp
from jax import lax
from jax.experimental import pallas as pl
from jax.experimental.pallas import tpu as pltpu

EPS = 1e-6
CHUNK = 64
GMLP_BLOCK = 128
GMLP_GROUPS = 8
ATT_HEADS = 16
ATT_LEFT_CHUNKS = 8
REL_CLIP = 256
TOP_K = 2
NEG_INF = -1e30
SUBLANES = 8
LANES = 128

VMEM_LIMIT_BYTES = 56 * 1024 * 1024
EXPERTS_VMEM_LIMIT_BYTES = 60 * 1024 * 1024
WEIGHT_STAGE_BYTES = 2 * 1024 * 1024

F32 = jnp.float32
BF16 = jnp.bfloat16


def _params(semantics):
    return pltpu.CompilerParams(dimension_semantics=semantics,
                                vmem_limit_bytes=VMEM_LIMIT_BYTES)


def _resident(shape):
    zeros = (0,) * len(shape)
    return pl.BlockSpec(shape, lambda *_: zeros, pipeline_mode=pl.Buffered(1))


_HBM = pl.BlockSpec(memory_space=pl.ANY)


def _stage_rows(w):
    rows = w.shape[0]
    while rows * w.shape[1] * 4 > WEIGHT_STAGE_BYTES and rows % 2 == 0:
        rows //= 2
    return rows


def _stage_scratch(w):
    return [pltpu.VMEM(w.shape, BF16), pltpu.VMEM((2, _stage_rows(w), w.shape[1]), F32),
            pltpu.SemaphoreType.DMA((2,))]


def _stream_rows(w_hbm, stage, sem, consume):
    rows = stage.shape[1]
    n_chunks = w_hbm.shape[0] // rows
    chunk = lambda c: pltpu.make_async_copy(w_hbm.at[pl.ds(c * rows, rows)], stage.at[c % 2],
                                            sem.at[c % 2])
    chunk(0).start()
    for c in range(n_chunks):
        if c + 1 < n_chunks:
            chunk(c + 1).start()
        chunk(c).wait()
        consume(slice(c * rows, (c + 1) * rows), stage[c % 2])


def _load_weight_bf16(w_hbm, w_bf, stage, sem, *, transpose=False):
    def consume(rows, chunk):
        if transpose:
            w_bf[:, rows] = chunk.T.astype(BF16)
        else:
            w_bf[rows, :] = chunk.astype(BF16)

    _stream_rows(w_hbm, stage, sem, consume)


def _rmsnorm(x, g):
    ms = jnp.mean(x * x, axis=-1, keepdims=True)
    return x * lax.rsqrt(ms + EPS) * g


def _gelu_exact(z):
    return 0.5 * z * (1.0 + lax.erf(z * (2.0 ** -0.5)))


def _gmlp_kernel(x_ref, g_ref, win_hbm, vgain_ref, ws_ref, bst_ref, wout_hbm,
                 o_ref, u_scr, v_scr, gated_scr, win_ref, win_stage, win_sem,
                 wout_ref, wout_stage, wout_sem, *, n_chunk):
    tm = x_ref.shape[0]
    half = u_scr.shape[1]
    group_dim = half // GMLP_GROUPS

    @pl.when(pl.program_id(0) == 0)
    def _():
        _load_weight_bf16(win_hbm, win_ref, win_stage, win_sem)
        _load_weight_bf16(wout_hbm, wout_ref, wout_stage, wout_sem)

    x = x_ref[...]
    h = _rmsnorm(x, g_ref[...]).astype(BF16)

    ssq = jnp.zeros((tm, 1), F32)
    for c in range(0, 2 * half, n_chunk):
        z = jnp.dot(h, win_ref[:, c:c + n_chunk], preferred_element_type=F32)
        z = _gelu_exact(z)
        if c < half:
            u_scr[:, c:c + n_chunk] = z
        else:
            v_scr[:, c - half:c - half + n_chunk] = z
            ssq = ssq + jnp.sum(z * z, axis=-1, keepdims=True)
    inv = lax.rsqrt(ssq * (1.0 / half) + EPS)

    row_chunk = lax.broadcasted_iota(jnp.int32, (GMLP_BLOCK, GMLP_BLOCK), 0) // CHUNK
    col_chunk = lax.broadcasted_iota(jnp.int32, (GMLP_BLOCK, GMLP_BLOCK), 1) // CHUNK
    causal = row_chunk >= col_chunk
    for g in range(GMLP_GROUPS):
        cols = slice(g * group_dim, (g + 1) * group_dim)
        w_g = jnp.where(causal, ws_ref[g], 0.0).astype(BF16)
        b_g = bst_ref[:, g:g + 1]
        for n in range(tm // GMLP_BLOCK):
            rows = slice(n * GMLP_BLOCK, (n + 1) * GMLP_BLOCK)
            vn = (v_scr[rows, cols] * inv[rows] * vgain_ref[:, cols]).astype(BF16)
            sv = jnp.dot(w_g, vn, preferred_element_type=F32) + b_g
            gated_scr[rows, cols] = (u_scr[rows, cols] * sv).astype(BF16)

    out = jnp.dot(gated_scr[...], wout_ref[...], preferred_element_type=F32)
    o_ref[...] = x + out


def _gmlp_layer(x2d, g, w_in, v_gain, w_s, b_s_t, w_out, *, tm=512, n_chunk=512):
    t, d = x2d.shape
    half = w_out.shape[0]
    return pl.pallas_call(
        functools.partial(_gmlp_kernel, n_chunk=n_chunk),
        out_shape=jax.ShapeDtypeStruct((t, d), F32),
        grid=(t // tm,),
        in_specs=[
            pl.BlockSpec((tm, d), lambda i: (i, 0)),
            _resident((1, d)),
            _HBM,
            _resident((1, half)),
            _resident(w_s.shape),
            _resident(b_s_t.shape),
            _HBM,
        ],
        out_specs=pl.BlockSpec((tm, d), lambda i: (i, 0)),
        scratch_shapes=[
            pltpu.VMEM((tm, half), F32),
            pltpu.VMEM((tm, half), F32),
            pltpu.VMEM((tm, half), BF16),
            *_stage_scratch(w_in), *_stage_scratch(w_out),
        ],
        compiler_params=_params(("arbitrary",)),
        name="gmlp_mixer",
    )(x2d, g, w_in, v_gain, w_s, b_s_t, w_out)


def _swiglu_kernel(x_ref, g_ref, wg_hbm, wu_hbm, wd_hbm, *rest, f_chunk, n_side):
    side_in, rest = rest[:n_side], rest[n_side:]
    o_ref, side_out, rest = rest[0], rest[1:1 + n_side], rest[1 + n_side:]
    (wg_ref, wg_stage, wg_sem, wu_ref, wu_stage, wu_sem, wd_ref, wd_stage, wd_sem), rest = (
        rest[:9], rest[9:])
    in_bufs, out_bufs = rest[:n_side], rest[n_side:2 * n_side]
    in_sem, out_sem = rest[2 * n_side:]
    i = pl.program_id(0)
    n = pl.num_programs(0)

    def side_rows(ref, buf, step):
        rows = buf.shape[0]
        return ref.at[pl.ds(pl.multiple_of(step * rows, rows), rows)]

    loads = [pltpu.make_async_copy(side_rows(side_in[j], in_bufs[j], i), in_bufs[j], in_sem.at[j])
             for j in range(n_side)]
    stores = lambda step: [pltpu.make_async_copy(out_bufs[j], side_rows(side_out[j], out_bufs[j],
                                                                         step), out_sem.at[j])
                           for j in range(n_side)]
    for cp in loads:
        cp.start()

    @pl.when(i == 0)
    def _():
        _load_weight_bf16(wg_hbm, wg_ref, wg_stage, wg_sem)
        _load_weight_bf16(wu_hbm, wu_ref, wu_stage, wu_sem)
        _load_weight_bf16(wd_hbm, wd_ref, wd_stage, wd_sem)

    x = x_ref[...]
    h = _rmsnorm(x, g_ref[...]).astype(BF16)
    o_ref[...] = x
    for c in range(0, wg_ref.shape[1], f_chunk):
        a = jnp.dot(h, wg_ref[:, c:c + f_chunk], preferred_element_type=F32)
        b = jnp.dot(h, wu_ref[:, c:c + f_chunk], preferred_element_type=F32)
        act = (jax.nn.silu(a) * b).astype(BF16)
        o_ref[...] += jnp.dot(act, wd_ref[c:c + f_chunk, :], preferred_element_type=F32)

    for cp in loads:
        cp.wait()

    @pl.when(i > 0)
    def _():
        for cp in stores(i - 1):
            cp.wait()

    for j in range(n_side):
        out_bufs[j][...] = in_bufs[j][...].astype(BF16)
    for cp in stores(i):
        cp.start()

    @pl.when(i == n - 1)
    def _():
        for cp in stores(i):
            cp.wait()


def _swiglu_layer(x2d, g, w_gate, w_up, w_down, side, *, tm=512, f_chunk=256):
    t, d = x2d.shape
    steps = t // tm
    slices = [(a.shape[0] // steps, a.shape[1]) for a in side]
    assert all(a.shape[0] % (steps * 16) == 0 for a in side)
    return pl.pallas_call(
        functools.partial(_swiglu_kernel, f_chunk=f_chunk, n_side=len(side)),
        out_shape=(jax.ShapeDtypeStruct((t, d), F32),
                   *[jax.ShapeDtypeStruct(a.shape, BF16) for a in side]),
        grid=(steps,),
        in_specs=[pl.BlockSpec((tm, d), lambda i: (i, 0)), _resident((1, d)), _HBM, _HBM, _HBM,
                  *[_HBM for _ in side]],
        out_specs=(pl.BlockSpec((tm, d), lambda i: (i, 0)), *[_HBM for _ in side]),
        scratch_shapes=[*_stage_scratch(w_gate), *_stage_scratch(w_up), *_stage_scratch(w_down),
                        *[pltpu.VMEM(s, F32) for s in slices],
                        *[pltpu.VMEM(s, BF16) for s in slices],
                        pltpu.SemaphoreType.DMA((len(side),)),
                        pltpu.SemaphoreType.DMA((len(side),))],
        compiler_params=_params(("arbitrary",)),
        name="dense_swiglu",
    )(x2d, g, w_gate, w_up, w_down, *side)


def _qkv_kernel(x_ref, g_ref, w_hbm, qt_ref, k_ref, vt_ref, wqt_ref, wk_ref, wvt_ref,
                stage, sem, *, q_scale):
    d = x_ref.shape[1]

    @pl.when(pl.program_id(0) == 0)
    def _():
        def consume(rows, chunk):
            wqt_ref[:, rows] = chunk[:, 0:d].T.astype(BF16)
            wk_ref[rows, :] = chunk[:, d:2 * d].astype(BF16)
            wvt_ref[:, rows] = chunk[:, 2 * d:3 * d].T.astype(BF16)

        _stream_rows(w_hbm, stage, sem, consume)

    h = _rmsnorm(x_ref[...], g_ref[...]).astype(BF16)
    nt_dims = (((1,), (1,)), ((), ()))
    qt = lax.dot_general(wqt_ref[...], h, nt_dims, preferred_element_type=F32)
    qt_ref[...] = (qt * q_scale).astype(BF16)
    k_ref[...] = jnp.dot(h, wk_ref[...], preferred_element_type=F32).astype(BF16)
    vt_ref[...] = lax.dot_general(wvt_ref[...], h, nt_dims,
                                  preferred_element_type=F32).astype(BF16)


def _qkv_layer(x2d, g, w_qkv, *, q_scale, tm=512):
    t, d = x2d.shape
    row_spec = pl.BlockSpec((tm, d), lambda i: (i, 0))
    col_spec = pl.BlockSpec((d, tm), lambda i: (0, i))
    return pl.pallas_call(
        functools.partial(_qkv_kernel, q_scale=q_scale),
        out_shape=(jax.ShapeDtypeStruct((d, t), BF16), jax.ShapeDtypeStruct((t, d), BF16),
                   jax.ShapeDtypeStruct((d, t), BF16)),
        grid=(t // tm,),
        in_specs=[row_spec, _resident((1, d)), _HBM],
        out_specs=(col_spec, row_spec, col_spec),
        scratch_shapes=[pltpu.VMEM((d, d), BF16), pltpu.VMEM((d, d), BF16),
                        pltpu.VMEM((d, d), BF16),
                        pltpu.VMEM((2, _stage_rows(w_qkv), 3 * d), F32),
                        pltpu.SemaphoreType.DMA((2,))],
        compiler_params=_params(("arbitrary",)),
        name="qkv_proj",
    )(x2d, g, w_qkv)


def _attn_kernel(x_ref, qt_ref, k0_ref, k1_ref, k2_ref, vt0_ref, vt1_ref, vt2_ref,
                 bias_ref, wo_hbm, o_ref, ctx_scr, wot_ref, wo_stage, wo_sem, *, head_dim):
    tq = qt_ref.shape[1]
    i = pl.program_id(1)

    @pl.when((pl.program_id(0) == 0) & (i == 0))
    def _():
        _load_weight_bf16(wo_hbm, wot_ref, wo_stage, wo_sem, transpose=True)

    k_refs = (k0_ref, k1_ref, k2_ref)
    vt_refs = (vt0_ref, vt1_ref, vt2_ref)
    n_win = len(k_refs) * tq
    half_q = tq // 2
    assert (ATT_LEFT_CHUNKS + 1) * CHUNK + half_q - CHUNK <= n_win - half_q
    key_pos = lax.broadcasted_iota(jnp.int32, (n_win, 1), 0) + (i - 2) * tq
    in_seq = key_pos >= 0
    zero_half = jnp.zeros((head_dim, tq), BF16)

    def scores(h):
        pair = slice((h // 2) * 2 * head_dim, (h // 2 + 1) * 2 * head_dim)
        qt_h = qt_ref[h * head_dim:(h + 1) * head_dim, :]
        rhs = jnp.concatenate([qt_h, zero_half] if h % 2 == 0 else [zero_half, qt_h], axis=0)
        return jnp.concatenate(
            [jnp.dot(kr[:, pair], rhs, preferred_element_type=F32) for kr in k_refs],
            axis=0)

    def all_heads(window_has_padding):
        s_next = scores(0)
        for h in range(ATT_HEADS):
            rows = slice(h * head_dim, (h + 1) * head_dim)
            s = s_next
            if h + 1 < ATT_HEADS:
                s_next = scores(h + 1)
            halves = []
            for lanes, keys in ((slice(0, half_q), slice(0, n_win - half_q)),
                                (slice(half_q, tq), slice(half_q, n_win))):
                sh = s[keys, lanes] + bias_ref[h, keys, lanes]
                if window_has_padding:
                    sh = jnp.where(in_seq[keys], sh, NEG_INF)
                m = jnp.max(sh, axis=0, keepdims=True)
                p = jnp.exp(sh - m)
                halves.append((jnp.sum(p, axis=0, keepdims=True), p.astype(BF16)))
            unseen = jnp.zeros((half_q, half_q), BF16)
            l = jnp.concatenate([halves[0][0], halves[1][0]], axis=1)
            pb = jnp.concatenate([jnp.concatenate([halves[0][1], unseen], axis=0),
                                  jnp.concatenate([unseen, halves[1][1]], axis=0)], axis=1)
            ctx = jnp.zeros((head_dim, tq), F32)
            for j, vr in enumerate(vt_refs):
                ctx = ctx + jnp.dot(vr[rows, :], pb[j * tq:(j + 1) * tq, :],
                                    preferred_element_type=F32)
            ctx_scr[rows, :] = (ctx / l).astype(BF16)
        out_t = jnp.dot(wot_ref[...], ctx_scr[...], preferred_element_type=F32)
        o_ref[...] = x_ref[...] + out_t.T

    pl.when(i < 2)(functools.partial(all_heads, True))
    pl.when(i >= 2)(functools.partial(all_heads, False))


def _attn_layer(x2d, q_t, k, v_t, bias_t, w_o, *, batch, tq=256):
    t, d = x2d.shape
    nt = t // batch // tq

    def cur(b, i):
        return b * nt + i

    def back(n):
        return lambda b, i: b * nt + jnp.maximum(i - n, 0)

    rows = lambda f: pl.BlockSpec((tq, d), lambda b, i: (f(b, i), 0))
    cols = lambda f: pl.BlockSpec((d, tq), lambda b, i: (0, f(b, i)))
    return pl.pallas_call(
        functools.partial(_attn_kernel, head_dim=d // ATT_HEADS),
        out_shape=jax.ShapeDtypeStruct((t, d), F32),
        grid=(batch, nt),
        in_specs=[rows(cur), cols(cur),
                  rows(back(2)), rows(back(1)), rows(cur),
                  cols(back(2)), cols(back(1)), cols(cur),
                  _resident(bias_t.shape), _HBM],
        out_specs=rows(cur),
        scratch_shapes=[pltpu.VMEM((d, tq), BF16), *_stage_scratch(w_o)],
        compiler_params=_params(("arbitrary", "arbitrary")),
        name="band_attention",
    )(x2d, q_t, k, k, k, v_t, v_t, v_t, bias_t, w_o)


def _band_bias(rel_table, tq):
    band = (ATT_LEFT_CHUNKS + 1) * CHUNK
    n_heads = rel_table.shape[0]
    assert 3 * tq >= band + tq - CHUNK and rel_table.shape[1] == REL_CLIP + CHUNK
    tail = jnp.broadcast_to(rel_table[:, -1:], (n_heads, band - 1 - REL_CLIP))
    rev = jnp.concatenate([rel_table, tail], axis=1).astype(F32)[:, ::-1]
    chunk = jnp.stack([rev[:, CHUNK - 1 - qi:CHUNK - 1 - qi + band] for qi in range(CHUNK)],
                      axis=1)
    lane_pad = -band % 128
    chunk = jnp.pad(chunk, ((0, 0), (0, 128 - CHUNK), (0, lane_pad)))
    return pl.pallas_call(
        functools.partial(_bias_tile_kernel, tq=tq, band=band),
        out_shape=jax.ShapeDtypeStruct((n_heads, 3 * tq, tq), F32),
        grid=(n_heads,),
        in_specs=[pl.BlockSpec((None, 128, band + lane_pad), lambda h: (h, 0, 0))],
        out_specs=pl.BlockSpec((None, 3 * tq, tq), lambda h: (h, 0, 0)),
        compiler_params=_params(("parallel",)),
        name="band_bias_tile",
    )(chunk)


def _bias_tile_kernel(chunk_ref, o_ref, *, tq, band):
    chunk_t = chunk_ref[...].T[:band, :CHUNK]
    cols = []
    for c in range(tq // CHUNK):
        parts = [chunk_t]
        if c > 0:
            parts.insert(0, jnp.full((c * CHUNK, CHUNK), NEG_INF, F32))
        below = 3 * tq - band - c * CHUNK
        if below > 0:
            parts.append(jnp.full((below, CHUNK), NEG_INF, F32))
        cols.append(jnp.concatenate(parts, axis=0))
    o_ref[...] = jnp.concatenate(cols, axis=1)


def _route_kernel(x_ref, g_ref, rw_ref, rb_ref, idx_ref, rank_ref, w_ref, cnt_ref, off_ref,
                  run_scr):
    tm = x_ref.shape[0]

    @pl.when(pl.program_id(0) == 0)
    def _():
        run_scr[...] = jnp.zeros_like(run_scr)

    off_ref[...] = run_scr[...].astype(jnp.int32)

    hf = _rmsnorm(x_ref[...], g_ref[...])
    n_e = rw_ref.shape[1]
    h_hi = hf.astype(BF16)
    h_lo = (hf - h_hi.astype(F32)).astype(BF16)
    rw = rw_ref[...]
    w_hi = rw.astype(BF16)
    w_lo = (rw - w_hi.astype(F32)).astype(BF16)
    by_hi = jnp.dot(h_hi, jnp.concatenate([w_hi, w_lo], axis=1), preferred_element_type=F32)
    logits = (by_hi[:, :n_e] + (by_hi[:, n_e:] + jnp.dot(h_lo, w_hi, preferred_element_type=F32))
              + rb_ref[...])
    lane = lax.broadcasted_iota(jnp.int32, logits.shape, 1)
    m1 = jnp.max(logits, axis=-1, keepdims=True)
    i1 = jnp.min(jnp.where(logits == m1, lane, n_e), axis=-1, keepdims=True)
    rest = jnp.where(lane == i1, -jnp.inf, logits)
    m2 = jnp.max(rest, axis=-1, keepdims=True)
    i2 = jnp.min(jnp.where(rest == m2, lane, n_e), axis=-1, keepdims=True)
    e2 = jnp.exp(m2 - m1)
    denom = 1.0 + e2

    sel = jnp.where((lane == i1) | (lane == i2), 1.0, 0.0)
    before = (lax.broadcasted_iota(jnp.int32, (tm, tm), 0)
              > lax.broadcasted_iota(jnp.int32, (tm, tm), 1))
    cum = jnp.dot(jnp.where(before, 1.0, 0.0).astype(BF16), sel.astype(BF16),
                  preferred_element_type=F32) + run_scr[...]
    r1 = jnp.sum(jnp.where(lane == i1, cum, 0.0), axis=-1, keepdims=True)
    r2 = jnp.sum(jnp.where(lane == i2, cum, 0.0), axis=-1, keepdims=True)
    run_scr[...] += jnp.sum(sel, axis=0, keepdims=True)

    idx_ref[...] = jnp.concatenate([i1, i2], axis=1)
    rank_ref[...] = jnp.concatenate([r1, r2], axis=1).astype(jnp.int32)
    w_ref[...] = jnp.concatenate([1.0 / denom, e2 / denom], axis=1)
    cnt_ref[...] = run_scr[...].astype(jnp.int32)


def _route(x2d, g, router_w, router_b, *, tm):
    t, d = x2d.shape
    n_e = router_w.shape[1]
    pair = pl.BlockSpec((tm, TOP_K), lambda i: (i, 0))
    return pl.pallas_call(
        _route_kernel,
        out_shape=(jax.ShapeDtypeStruct((t, TOP_K), jnp.int32),
                   jax.ShapeDtypeStruct((t, TOP_K), jnp.int32),
                   jax.ShapeDtypeStruct((t, TOP_K), F32),
                   jax.ShapeDtypeStruct((1, n_e), jnp.int32),
                   jax.ShapeDtypeStruct((t // tm, 1, n_e), jnp.int32)),
        grid=(t // tm,),
        in_specs=[pl.BlockSpec((tm, d), lambda i: (i, 0)), _resident((1, d)),
                  _resident(router_w.shape), _resident((1, n_e))],
        out_specs=(pair, pair, pair, pl.BlockSpec((1, n_e), lambda i: (0, 0)),
                   pl.BlockSpec((None, 1, n_e), lambda i: (i, 0, 0))),
        scratch_shapes=[pltpu.VMEM((1, n_e), F32)],
        compiler_params=_params(("arbitrary",)),
        name="moe_route",
    )(x2d, g, router_w, router_b)


def _run_pieces(lstart_ref, len_ref, sstart_ref, blk, *, n_e, max_run, visit):
    for e in range(n_e):
        lstart = lstart_ref[blk * n_e + e]
        run_len = len_ref[blk * n_e + e]
        sstart = sstart_ref[blk * n_e + e]
        for log_len in range(max_run.bit_length() - 1, SUBLANES.bit_length() - 2, -1):
            length = 1 << log_len
            done = (run_len >> (log_len + 1)) << (log_len + 1)

            @pl.when((run_len & length) != 0)
            def _(lstart=lstart, sstart=sstart, done=done, length=length):
                visit(pl.multiple_of(lstart + done, SUBLANES),
                      pl.multiple_of(sstart + done, SUBLANES), length)


def _dispatch_kernel(lstart_ref, cnt_ref, gdst_ref, padst_ref, nused_ref, x_ref, g_ref, lpos_ref,
                     w_ref, hs_ref, buf, zero_scr, sem, zero_sem, *, n_e, row_tile,
                     min_used_tiles):
    i = pl.program_id(0)
    n = pl.num_programs(0)
    tb = x_ref.shape[0]
    n_loc = buf.shape[1]
    slot = lax.rem(i, 2)

    def run_copies(blk, s, act):
        _run_pieces(lstart_ref, cnt_ref, gdst_ref, blk, n_e=n_e, max_run=tb + SUBLANES,
                    visit=lambda lo, so, length: act(pltpu.make_async_copy(
                        buf.at[s, pl.ds(lo, length)], hs_ref.at[pl.ds(so, length)], sem.at[s])))

    @pl.when(i == 0)
    def _():
        zero_scr[...] = jnp.zeros_like(zero_scr)
        pads = [pltpu.make_async_copy(
            zero_scr, hs_ref.at[pl.ds(pl.multiple_of(padst_ref[e], SUBLANES), row_tile)], zero_sem)
            for e in range(n_e)]
        for cp in pads:
            cp.start()
        for cp in pads:
            cp.wait()
        for act in (lambda cp: cp.start(), lambda cp: cp.wait()):
            for j in range(min_used_tiles, hs_ref.shape[0] // row_tile):
                @pl.when(j >= nused_ref[0])
                def _(j=j, act=act):
                    act(pltpu.make_async_copy(
                        zero_scr, hs_ref.at[pl.ds(j * row_tile, row_tile)], zero_sem))

    @pl.when(i >= 2)
    def _():
        run_copies(i - 2, slot, lambda cp: cp.wait())

    d = x_ref.shape[1]
    h = _rmsnorm(x_ref[...], g_ref[...]).astype(BF16)
    row = lax.broadcasted_iota(jnp.int32, (n_loc, tb), 0)
    lpos = lpos_ref[...]
    one_hot = lambda picks: jnp.where(picks, 1.0, 0.0).astype(BF16)
    picks = [row == lpos[k:k + 1, :] for k in range(TOP_K)]
    buf[slot, :, 0:d] = jnp.dot(one_hot(picks[0] | picks[1]), h, preferred_element_type=F32)
    lane = lax.broadcasted_iota(jnp.int32, (tb, LANES), 1)
    gates = jnp.zeros((n_loc, LANES), F32)
    for k in range(TOP_K):
        rest = w_ref[:, k:k + 1]
        pieces = jnp.zeros((tb, LANES), F32)
        for j in range(3):
            piece = rest.astype(BF16).astype(F32)
            rest = rest - piece
            pieces = jnp.where(lane == j, piece, pieces)
        gates = gates + jnp.dot(one_hot(picks[k]), pieces.astype(BF16),
                                preferred_element_type=F32)
    buf[slot, :, d:d + LANES] = gates
    run_copies(i, slot, lambda cp: cp.start())

    @pl.when(i == n - 1)
    def _():
        run_copies(i, slot, lambda cp: cp.wait())

    @pl.when((i == n - 1) & (n >= 2))
    def _():
        run_copies(i - 1, 1 - slot, lambda cp: cp.wait())


def _dispatch(lstart, cnt, gdst, pad_start, n_used, x2d, g, lpos_t, w, n_rows, *, tb, n_e,
              row_tile):
    t, d = x2d.shape
    width = d + LANES
    return pl.pallas_call(
        functools.partial(_dispatch_kernel, n_e=n_e, row_tile=row_tile,
                          min_used_tiles=(TOP_K * t) // row_tile),
        out_shape=jax.ShapeDtypeStruct((n_rows + row_tile, width), F32),
        grid_spec=pltpu.PrefetchScalarGridSpec(
            num_scalar_prefetch=5,
            grid=(t // tb,),
            in_specs=[pl.BlockSpec((tb, d), lambda i, *_: (i, 0)),
                      pl.BlockSpec((1, d), lambda i, *_: (0, 0)),
                      pl.BlockSpec((TOP_K, tb), lambda i, *_: (0, i)),
                      pl.BlockSpec((tb, TOP_K), lambda i, *_: (i, 0))],
            out_specs=pl.BlockSpec(memory_space=pl.ANY),
            scratch_shapes=[pltpu.VMEM((2, _block_rows(tb, n_e), width), F32),
                            pltpu.VMEM((row_tile, width), F32),
                            pltpu.SemaphoreType.DMA((2,)), pltpu.SemaphoreType.DMA(())],
        ),
        compiler_params=_params(("arbitrary",)),
        name="moe_dispatch",
    )(lstart, cnt, gdst, pad_start, n_used, x2d, g, lpos_t, w)


def _block_rows(tb, n_e):
    return TOP_K * tb + SUBLANES * n_e


def _experts_kernel(te_ref, nused_ref, hs_ref, wg_ref, wu_ref, wd_ref, y_ref, *, f_chunk):
    del te_ref
    i = pl.program_id(0)

    @pl.when(i < nused_ref[0])
    def _():
        d = y_ref.shape[1]
        f_dim = wg_ref.shape[1]
        h = hs_ref[:, 0:d].astype(BF16)
        gate = jnp.sum(hs_ref[:, d:], axis=-1, keepdims=True)
        for c in range(0, f_dim, f_chunk):
            a = jnp.dot(h, wg_ref[:, c:c + f_chunk], preferred_element_type=F32)
            b = jnp.dot(h, wu_ref[:, c:c + f_chunk], preferred_element_type=F32)
            act = (jax.nn.silu(a) * b).astype(BF16)
            y = jnp.dot(act, wd_ref[c:c + f_chunk, :], preferred_element_type=F32)
            if c == 0:
                y_ref[...] = y
            elif c + f_chunk < f_dim:
                y_ref[...] += y
            else:
                y_ref[...] = gate * (y_ref[...] + y)

    @pl.when(i >= nused_ref[0])
    def _():
        y_ref[...] = jnp.zeros_like(y_ref)


def _experts(tile_expert, n_used, hs, w_gate, w_up, w_down, *, tm, f_chunk=256):
    d = w_gate.shape[1]
    n_rows = tile_expert.shape[0] * tm
    f_dim = w_gate.shape[2]

    def row_blk(i, te, nu):
        return (jnp.maximum(jnp.minimum(i, nu[0] - 1), 0), 0)

    expert_blk = lambda i, te, nu: (te[i], 0, 0)
    return pl.pallas_call(
        functools.partial(_experts_kernel, f_chunk=f_chunk),
        out_shape=jax.ShapeDtypeStruct((n_rows, d), F32),
        grid_spec=pltpu.PrefetchScalarGridSpec(
            num_scalar_prefetch=2,
            grid=(n_rows // tm,),
            in_specs=[
                pl.BlockSpec((tm, hs.shape[1]), row_blk),
                pl.BlockSpec((None, d, f_dim), expert_blk),
                pl.BlockSpec((None, d, f_dim), expert_blk),
                pl.BlockSpec((None, f_dim, d), expert_blk),
            ],
            out_specs=pl.BlockSpec((tm, d), lambda i, te, nu: (i, 0)),
        ),
        compiler_params=pltpu.CompilerParams(dimension_semantics=("arbitrary",),
                                             vmem_limit_bytes=EXPERTS_VMEM_LIMIT_BYTES),
        name="moe_experts",
    )(tile_expert, n_used, hs, w_gate, w_up, w_down)


def _combine_kernel(lstart_ref, cnt_ref, gsrc_ref, x_ref, lpos_t_ref, fg_ref, y_ref, o_ref,
                    buf, sem, *, n_e):
    i = pl.program_id(0)
    n = pl.num_programs(0)
    tb = x_ref.shape[0]
    n_loc = buf.shape[1]
    slot = lax.rem(i, 2)

    def run_copies(blk, s, act):
        _run_pieces(lstart_ref, cnt_ref, gsrc_ref, blk, n_e=n_e, max_run=tb + SUBLANES,
                    visit=lambda lo, so, length: act(pltpu.make_async_copy(
                        y_ref.at[pl.ds(so, length)], buf.at[s, pl.ds(lo, length)], sem.at[s])))

    def fetch(blk, s):
        buf[s] = jnp.zeros(buf.shape[1:], F32)
        run_copies(blk, s, lambda cp: cp.start())

    @pl.when(i == 0)
    def _():
        fetch(0, 0)

    @pl.when(i + 1 < n)
    def _():
        fetch(i + 1, 1 - slot)

    run_copies(i, slot, lambda cp: cp.wait())

    y = buf[slot]
    y_hi = y.astype(BF16)
    y_lo = (y - y_hi.astype(F32)).astype(BF16)
    col = lax.broadcasted_iota(jnp.int32, (tb, n_loc), 1)
    lpos_t = lpos_t_ref[...]
    picks = (col == lpos_t[:, 0:1]) | (col == lpos_t[:, 1:2])
    one_hot = jnp.where(picks, 1.0, 0.0).astype(BF16)
    moe = (jnp.dot(one_hot, y_hi, preferred_element_type=F32)
           + jnp.dot(one_hot, y_lo, preferred_element_type=F32))
    o_ref[...] = _rmsnorm(x_ref[...] + moe, fg_ref[...])


def _combine(lstart, cnt, gsrc, x2d, lpos, final_g, y_sorted, *, tb, n_e):
    t, d = x2d.shape
    return pl.pallas_call(
        functools.partial(_combine_kernel, n_e=n_e),
        out_shape=jax.ShapeDtypeStruct((t, d), F32),
        grid_spec=pltpu.PrefetchScalarGridSpec(
            num_scalar_prefetch=3,
            grid=(t // tb,),
            in_specs=[pl.BlockSpec((tb, d), lambda i, *_: (i, 0)),
                      pl.BlockSpec((tb, TOP_K), lambda i, *_: (i, 0)),
                      pl.BlockSpec((1, d), lambda i, *_: (0, 0)),
                      pl.BlockSpec(memory_space=pl.ANY)],
            out_specs=pl.BlockSpec((tb, d), lambda i, *_: (i, 0)),
            scratch_shapes=[pltpu.VMEM((2, _block_rows(tb, n_e), d), F32),
                            pltpu.SemaphoreType.DMA((2,))],
        ),
        compiler_params=_params(("arbitrary",)),
        name="moe_combine",
    )(lstart, cnt, gsrc, x2d, lpos, final_g, y_sorted)


def _moe_layer(x2d, g, router_w, router_b, w_gate, w_up, w_down, final_g, *,
               row_tile=512, tok_block=512):
    t, d = x2d.shape
    n_e = router_w.shape[1]
    n_blk = t // tok_block
    idx, rank, w, counts, blk_off = _route(x2d, g, router_w, router_b, tm=tok_block)

    counts = counts[0]
    blk_off = blk_off[:, 0, :]
    blk_cnt = jnp.concatenate([blk_off[1:], counts[None]], axis=0) - blk_off
    run_len = (blk_cnt + SUBLANES - 1) // SUBLANES * SUBLANES
    run_lstart = jnp.cumsum(run_len, axis=1) - run_len
    group_len = jnp.sum(run_len, axis=0)
    padded = (group_len + row_tile - 1) // row_tile * row_tile
    ends = jnp.cumsum(padded)
    starts = ends - padded
    run_gdst = starts[None, :] + jnp.cumsum(run_len, axis=0) - run_len

    experts = jnp.arange(n_e, dtype=jnp.int32)
    chosen_b = (idx[..., None] == experts).reshape(n_blk, tok_block, TOP_K, n_e)
    per_tok = lambda v: jnp.sum(jnp.where(chosen_b, v[:, None, None, :], 0), axis=-1)
    in_run = rank.reshape(n_blk, tok_block, TOP_K) - per_tok(blk_off)
    lpos = (in_run + per_tok(run_lstart)).reshape(t, TOP_K).astype(jnp.int32)

    n_tiles = pl.cdiv(TOP_K * t + (SUBLANES - 1) * n_blk * n_e, row_tile) + n_e
    tile_ends = ends // row_tile
    n_used = tile_ends[-1:].astype(jnp.int32)
    tile_ids = jnp.arange(n_tiles, dtype=jnp.int32)
    tile_expert = jnp.sum(jnp.minimum(tile_ids, n_used - 1)[:, None] >= tile_ends[None, :],
                          axis=1).astype(jnp.int32)

    flat = lambda v: v.reshape(-1).astype(jnp.int32)
    runs = (flat(run_lstart), flat(run_len), flat(run_gdst))
    hs = _dispatch(*runs, flat(starts + group_len), n_used, x2d, g, lpos.T, w,
                   n_tiles * row_tile, tb=tok_block, n_e=n_e, row_tile=row_tile)
    y_sorted = _experts(tile_expert, n_used, hs, w_gate, w_up, w_down, tm=row_tile)
    return _combine(*runs, x2d, lpos, final_g, y_sorted, tb=tok_block, n_e=n_e)


def kernel(x, norm_mix_g, norm_ffn_g, final_g, a_w_in, a_v_gain, a_w_s, a_b_s, a_w_out,
           a_ffn_gate, a_ffn_up, a_ffn_down, b_w_qkv, b_rel_bias, b_w_o, b_router_w,
           b_router_b, b_exp_gate, b_exp_up, b_exp_down):
    batch, seq, d = x.shape
    assert norm_mix_g.shape[0] == 2, "two layers: gMLP/SwiGLU then attention/MoE"
    head_dim = d // ATT_HEADS
    tq = 256
    xf = x.reshape(batch * seq, d)
    row = lambda vec: vec.reshape(1, -1).astype(F32)

    xf = _gmlp_layer(xf, row(norm_mix_g[0]), a_w_in[0], row(a_v_gain[0]),
                     a_w_s[0], a_b_s[0].T, a_w_out[0])
    n_e, _, f_dim = b_exp_gate.shape[1:]
    xf, exp_gate, exp_up, exp_down = _swiglu_layer(
        xf, row(norm_ffn_g[0]), a_ffn_gate[0], a_ffn_up[0], a_ffn_down[0],
        [b_exp_gate[0].reshape(n_e * d, f_dim), b_exp_up[0].reshape(n_e * d, f_dim),
         b_exp_down[0].reshape(n_e * f_dim, d)])

    q_t, k, v_t = _qkv_layer(xf, row(norm_mix_g[1]), b_w_qkv[0], q_scale=head_dim ** -0.5)
    bias_t = _band_bias(b_rel_bias[0], tq)
    xf = _attn_layer(xf, q_t, k, v_t, bias_t, b_w_o[0], batch=batch, tq=tq)
    xf = _moe_layer(xf, row(norm_ffn_g[1]), b_router_w[0], row(b_router_b[0]),
                    exp_gate.reshape(n_e, d, f_dim), exp_up.reshape(n_e, d, f_dim),
                    exp_down.reshape(n_e, f_dim, d), row(final_g))
    return xf.reshape(batch, seq, d)
```
